```python
import math, functools
import jax, jax.numpy as jnp
from jax import lax
import numpy as np

D_MODEL = 1024
BATCH = 4
SEQ = 8192
DEPTH = 1
DEC_BATCH = 32
DEC_SEQ = 64
PAST_LEN = 1024

CHUNK = 64
N_HEADS = 8
HEAD_DIM = 64
V_DIM = 2 * HEAD_DIM
Q_WIDTH = N_HEADS * 2 * HEAD_DIM
ATTN_WIDTH = N_HEADS * V_DIM
LRU_WIDTH = D_MODEL
LRU_BLOCKS = 16
LRU_BLOCK = LRU_WIDTH // LRU_BLOCKS
CONV_WIDTH = 4
LRU_C = 8.0
D_FF = 4 * D_MODEL
Q_BLOCK = 128
EPS = 1e-6
IN_WIDTH = 3 * Q_WIDTH + 2 * LRU_WIDTH + 2 * D_MODEL
IN_SPLITS = (Q_WIDTH, 2 * Q_WIDTH, 3 * Q_WIDTH, 3 * Q_WIDTH + LRU_WIDTH,
             3 * Q_WIDTH + 2 * LRU_WIDTH, 3 * Q_WIDTH + 2 * LRU_WIDTH + D_MODEL)

kernel_name = "diffattn_rglru_gated_streaming_encoder"


def _rms_norm(x, g):
    xf = x.astype(jnp.float32)
    xf = xf * lax.rsqrt(jnp.mean(jnp.square(xf), axis=-1, keepdims=True) + EPS)
    return (xf * g.astype(jnp.float32)).astype(x.dtype)


def _diff_attention(q, k, v, q_pos, k_pos, lam, head_gain, lam_init):
    scale = HEAD_DIM ** -0.5
    s = jnp.einsum("bqhcd,bkhcd->bhcqk", q.astype(jnp.float32), k.astype(jnp.float32)) * scale
    visible = (k_pos[None, :] // CHUNK) <= (q_pos[:, None] // CHUNK)
    s = jnp.where(visible, s, -jnp.inf)
    p = jax.nn.softmax(s, axis=-1)
    a = p[:, :, 0] - lam * p[:, :, 1]
    o = jnp.einsum("bhqk,bkhe->bqhe", a, v.astype(jnp.float32))
    o = o * lax.rsqrt(jnp.mean(jnp.square(o), axis=-1, keepdims=True) + EPS)
    return o * head_gain.astype(jnp.float32) * (1.0 - lam_init)


def _attend_prompt(q, k, v, lam, head_gain, lam_init):
    b, s = q.shape[0], q.shape[1]
    nb = s // Q_BLOCK
    qb = q.reshape(b, nb, Q_BLOCK, N_HEADS, 2, HEAD_DIM).swapaxes(0, 1)
    k_pos = jnp.arange(s)

    def one_block(args):
        q_blk, start = args
        q_pos = start + jnp.arange(Q_BLOCK)
        return _diff_attention(q_blk, k, v, q_pos, k_pos, lam, head_gain, lam_init)

    o = lax.map(one_block, (qb, jnp.arange(nb) * Q_BLOCK))
    return o.swapaxes(0, 1).reshape(b, s, ATTN_WIDTH)


def _attend_sample(cache_k, cache_v, q, k, v, lam, head_gain, lam_init):
    b, s = q.shape[0], q.shape[1]
    past = cache_k.shape[1]
    kc = jnp.concatenate([cache_k.astype(k.dtype), k], axis=1)
    vc = jnp.concatenate([cache_v.astype(v.dtype), v], axis=1)
    q_pos = past + jnp.arange(s)
    k_pos = jnp.arange(past + s)
    o = _diff_attention(q, kc, vc, q_pos, k_pos, lam, head_gain, lam_init)
    return o.reshape(b, s, ATTN_WIDTH)


def _lin_combine(left, right):
    a1, b1 = left
    a2, b2 = right
    return a1 * a2, a2 * b1 + b2


def _rglru_branch(x_lru, g_lru, conv_state, h0, conv_w, conv_b, w_r, b_r, w_i, b_i, lru_lambda):
    b, s, _ = x_lru.shape
    xpad = jnp.concatenate([conv_state.astype(x_lru.dtype), x_lru], axis=1)
    xc = conv_b + sum(conv_w[j] * xpad[:, j:j + s] for j in range(CONV_WIDTH))
    new_conv = xpad[:, -(CONV_WIDTH - 1):]
    xb = xc.reshape(b, s, LRU_BLOCKS, LRU_BLOCK)
    r = jax.nn.sigmoid(jnp.einsum("bsnc,ncd->bsnd", xb, w_r).reshape(b, s, LRU_WIDTH) + b_r)
    i = jax.nn.sigmoid(jnp.einsum("bsnc,ncd->bsnd", xb, w_i).reshape(b, s, LRU_WIDTH) + b_i)
    log_a = -LRU_C * r.astype(jnp.float32) * jax.nn.softplus(-lru_lambda.astype(jnp.float32))
    a = jnp.exp(log_a)
    u = jnp.sqrt(-jnp.expm1(2.0 * log_a)) * (i * xc).astype(jnp.float32)
    a_cum, h = lax.associative_scan(_lin_combine, (a, u), axis=1)
    h = h + a_cum * h0.astype(jnp.float32)[:, None, :]
    y = h * jax.nn.gelu(g_lru.astype(jnp.float32))
    return y, new_conv, h[:, -1]


def _layer(x, conv_state, h0, attend, g_mix, g_mlp, w_in, lam_q, lam_k, head_gain,
           conv_w, conv_b, w_r, b_r, w_i, b_i, lru_lambda, w_ba, w_bl, w_o, w_up, w_down, lam_init):
    b, s, _ = x.shape
    xn = _rms_norm(x, g_mix)
    proj = xn @ w_in
    q, k, v, x_lru, g_lru, gate_a, gate_b = jnp.split(proj, IN_SPLITS, axis=-1)
    q = q.reshape(b, s, N_HEADS, 2, HEAD_DIM)
    k = k.reshape(b, s, N_HEADS, 2, HEAD_DIM)
    v = v.reshape(b, s, N_HEADS, V_DIM)
    lqf, lkf = lam_q.astype(jnp.float32), lam_k.astype(jnp.float32)
    lam = jnp.exp(jnp.sum(lqf[0] * lkf[0])) - jnp.exp(jnp.sum(lqf[1] * lkf[1])) + lam_init
    o_attn = attend(q, k, v, lam, head_gain, lam_init)
    y_lru, conv_new, h_new = _rglru_branch(x_lru, g_lru, conv_state, h0, conv_w, conv_b,
                                           w_r, b_r, w_i, b_i, lru_lambda)
    merged = (jax.nn.sigmoid(gate_a.astype(jnp.float32)) * (o_attn.astype(x.dtype) @ w_ba)
              + jax.nn.sigmoid(gate_b.astype(jnp.float32)) * (y_lru.astype(x.dtype) @ w_bl))
    h = x + (merged.astype(x.dtype) @ w_o).astype(x.dtype)
    hn = _rms_norm(h, g_mlp)
    out = h + (jnp.square(jax.nn.relu(hn @ w_up)) @ w_down).astype(x.dtype)
    return out, k, v, conv_new, h_new.astype(x.dtype)


def setup_inputs(seed: int = 0) -> dict:
    key = jax.random.key(seed)
    ks = jax.random.split(key, 32)
    f32 = jnp.float32
    nrm = lambda k, shape, sc: jax.random.normal(k, shape, f32) * sc
    a_c = jax.random.uniform(ks[17], (DEPTH, LRU_WIDTH), f32, 0.9, 0.999)
    s_l = a_c ** (1.0 / LRU_C)
    lru_lambda = jnp.log(s_l) - jnp.log1p(-s_l)
    return {
        "x_prompt": nrm(ks[0], (BATCH, SEQ, D_MODEL), 1.0),
        "x_sample": nrm(ks[1], (DEC_BATCH, DEC_SEQ, D_MODEL), 1.0),
        "cache_k": nrm(ks[2], (DEPTH, DEC_BATCH, PAST_LEN, N_HEADS, 2, HEAD_DIM), 1.0),
        "cache_v": nrm(ks[3], (DEPTH, DEC_BATCH, PAST_LEN, N_HEADS, V_DIM), 1.0),
        "state_conv": nrm(ks[4], (DEPTH, DEC_BATCH, CONV_WIDTH - 1, LRU_WIDTH), 1.0),
        "state_lru": nrm(ks[5], (DEPTH, DEC_BATCH, LRU_WIDTH), 0.5),
        "norm_mix": 1.0 + nrm(ks[6], (DEPTH, D_MODEL), 0.02),
        "norm_mlp": 1.0 + nrm(ks[7], (DEPTH, D_MODEL), 0.02),
        "norm_final": 1.0 + nrm(ks[8], (D_MODEL,), 0.02),
        "w_in": nrm(ks[9], (DEPTH, D_MODEL, IN_WIDTH), D_MODEL ** -0.5),
        "lambda_q": nrm(ks[10], (DEPTH, 2, HEAD_DIM), 0.1),
        "lambda_k": nrm(ks[11], (DEPTH, 2, HEAD_DIM), 0.1),
        "head_gain": 1.0 + nrm(ks[12], (DEPTH, V_DIM), 0.02),
        "conv_w": nrm(ks[13], (DEPTH, CONV_WIDTH, LRU_WIDTH), CONV_WIDTH ** -0.5),
        "conv_b": nrm(ks[14], (DEPTH, LRU_WIDTH), 0.01),
        "w_rgate": nrm(ks[15], (DEPTH, LRU_BLOCKS, LRU_BLOCK, LRU_BLOCK), LRU_BLOCK ** -0.5),
        "b_rgate": nrm(ks[16], (DEPTH, LRU_WIDTH), 0.01),
        "w_igate": nrm(ks[18], (DEPTH, LRU_BLOCKS, LRU_BLOCK, LRU_BLOCK), LRU_BLOCK ** -0.5),
        "b_igate": nrm(ks[19], (DEPTH, LRU_WIDTH), 0.01),
        "lru_lambda": lru_lambda,
        "w_branch_attn": nrm(ks[20], (DEPTH, ATTN_WIDTH, D_MODEL), ATTN_WIDTH ** -0.5),
        "w_branch_lru": nrm(ks[21], (DEPTH, LRU_WIDTH, D_MODEL), LRU_WIDTH ** -0.5),
        "w_out": nrm(ks[22], (DEPTH, D_MODEL, D_MODEL), D_MODEL ** -0.5),
        "w_mlp_up": nrm(ks[23], (DEPTH, D_MODEL, D_FF), D_MODEL ** -0.5),
        "w_mlp_down": nrm(ks[24], (DEPTH, D_FF, D_MODEL), D_FF ** -0.5),
    }


def reference(x_prompt, x_sample, cache_k, cache_v, state_conv, state_lru,
              norm_mix, norm_mlp, norm_final, w_in, lambda_q, lambda_k, head_gain,
              conv_w, conv_b, w_rgate, b_rgate, w_igate, b_igate, lru_lambda,
              w_branch_attn, w_branch_lru, w_out, w_mlp_up, w_mlp_down):
    xp, xs = x_prompt, x_sample
    bp = xp.shape[0]
    kp_l, vp_l, cp_l, hp_l, ks_l, vs_l, cs_l, hs_l = [], [], [], [], [], [], [], []
    for l in range(DEPTH):
        lam_init = 0.8 - 0.6 * math.exp(-0.3 * l)
        lw = (norm_mix[l], norm_mlp[l], w_in[l], lambda_q[l], lambda_k[l], head_gain[l],
              conv_w[l], conv_b[l], w_rgate[l], b_rgate[l], w_igate[l], b_igate[l], lru_lambda[l],
              w_branch_attn[l], w_branch_lru[l], w_out[l], w_mlp_up[l], w_mlp_down[l], lam_init)
        conv0 = jnp.zeros((bp, CONV_WIDTH - 1, LRU_WIDTH), xp.dtype)
        h0 = jnp.zeros((bp, LRU_WIDTH), xp.dtype)
        xp, kp, vp, cp, hp = _layer(xp, conv0, h0, _attend_prompt, *lw)
        attend_s = functools.partial(_attend_sample, cache_k[l], cache_v[l])
        xs, kn, vn, cn, hn = _layer(xs, state_conv[l], state_lru[l], attend_s, *lw)
        kp_l.append(kp); vp_l.append(vp); cp_l.append(cp); hp_l.append(hp)
        ks_l.append(kn); vs_l.append(vn); cs_l.append(cn); hs_l.append(hn)
    y_prompt = _rms_norm(xp, norm_final)
    y_sample = _rms_norm(xs, norm_final)
    k_prompt = jnp.stack(kp_l, axis=0)
    v_prompt = jnp.stack(vp_l, axis=0)
    conv_prompt = jnp.stack(cp_l, axis=0)
    lru_prompt = jnp.stack(hp_l, axis=0)
    k_sample = jnp.stack(ks_l, axis=0)
    v_sample = jnp.stack(vs_l, axis=0)
    conv_sample = jnp.stack(cs_l, axis=0)
    lru_sample = jnp.stack(hs_l, axis=0)
    return (y_prompt, y_sample, k_prompt, v_prompt, conv_prompt, lru_prompt,
            k_sample, v_sample, conv_sample, lru_sample)
```

```python
import functools
import math

import jax
import jax.numpy as jnp
import numpy as np
from jax import lax
from jax.experimental import pallas as pl
from jax.experimental.pallas import tpu as pltpu

D_MODEL = 1024
N_HEADS = 8
HEAD_DIM = 64
V_DIM = 2 * HEAD_DIM
CHUNK = 64
LRU_WIDTH = D_MODEL
LRU_BLOCKS = 16
LRU_BLOCK = LRU_WIDTH // LRU_BLOCKS
CONV_WIDTH = 4
LRU_C = 8.0
D_FF = 4 * D_MODEL
EPS = 1e-6

SUBLANES = 8
LANES = 128
MXU_DEPTH = 256
GATE_GROUP = MXU_DEPTH // LRU_BLOCK
N_GATE_GROUPS = LRU_BLOCKS // GATE_GROUP
CONV_PAD = SUBLANES
MASK_VALUE = -0.7 * float(np.finfo(np.float32).max)
VMEM_LIMIT_BYTES = 56 * 1024 * 1024

F32 = jnp.float32
BF16 = jnp.bfloat16


def _rms_norm_f32(x, g):
    return x * lax.rsqrt(jnp.mean(x * x, axis=-1, keepdims=True) + EPS) * g


def _const_spec(shape):
    zeros = (0,) * len(shape)
    return pl.BlockSpec(shape, lambda *_: zeros, pipeline_mode=pl.Buffered(1))


def _proj_kernel(x_ref, conv0_ref, h0_ref, gmix_ref, win_ref, wg_ref, bg_ref, convw_ref, convb_ref,
                 lam_ref, wbl_ref,
                 q_ref, k_ref, v_ref, kb_ref, vb_ref, ga_ref, lru_ref, convout_ref, hout_ref,
                 xn_scr, xpad_scr, gl_scr, a_scr, u_scr, hc_scr,
                 *, tm, seg_len, tiles_per_seq):
    nseg = tm // seg_len
    i = pl.program_id(0)

    x = x_ref[...]
    xn_scr[...] = _rms_norm_f32(x, gmix_ref[...]).astype(BF16)

    def proj(c):
        return jnp.dot(xn_scr[...], win_ref[:, c * D_MODEL:(c + 1) * D_MODEL],
                       preferred_element_type=F32)

    q_ref[...] = (proj(0) * (HEAD_DIM ** -0.5)).astype(BF16)
    kk = proj(1)
    k_ref[...] = kk
    kb_ref[...] = kk.astype(BF16)
    vv = proj(2)
    v_ref[...] = vv
    vb_ref[...] = vv.astype(BF16)
    xl = proj(3)
    for j in range(nseg):
        xpad_scr[j, CONV_PAD:CONV_PAD + seg_len, :] = xl[j * seg_len:(j + 1) * seg_len]
    gl_scr[...] = jax.nn.gelu(proj(4))
    ga_ref[...] = jax.nn.sigmoid(proj(5))
    lru_ref[...] = jax.nn.sigmoid(proj(6))

    hist = slice(CONV_PAD - (CONV_WIDTH - 1), CONV_PAD)
    tail = slice(CONV_PAD + seg_len - (CONV_WIDTH - 1), CONV_PAD + seg_len)
    if tiles_per_seq > 1:
        @pl.when(i % tiles_per_seq == 0)
        def _():
            xpad_scr[0, hist, :] = conv0_ref[0]
            hc_scr[0:1, :] = h0_ref[0]
    else:
        for j in range(nseg):
            xpad_scr[j, hist, :] = conv0_ref[j]

    convw = convw_ref[...]
    xc_parts = []
    for j in range(nseg):
        acc = convb_ref[...] + convw[0:1, :] * xpad_scr[j, pl.ds(CONV_PAD - 3, seg_len), :]
        for t in range(1, CONV_WIDTH):
            acc = acc + convw[t:t + 1, :] * xpad_scr[j, pl.ds(CONV_PAD - 3 + t, seg_len), :]
        xc_parts.append(acc)
    xc = xc_parts[0] if nseg == 1 else jnp.concatenate(xc_parts, axis=0)

    xcb = xc.astype(BF16)

    def gate(which):
        cols = [jnp.dot(xcb[:, g * MXU_DEPTH:(g + 1) * MXU_DEPTH], wg_ref[which, g],
                        preferred_element_type=F32) for g in range(N_GATE_GROUPS)]
        return jax.nn.sigmoid(jnp.concatenate(cols, axis=1) + bg_ref[which:which + 1, :])

    r = gate(0)
    ig = gate(1)
    z = -lam_ref[...]
    softplus = jnp.maximum(z, 0.0) + jnp.log1p(jnp.exp(-jnp.abs(z)))
    log_a = (-LRU_C * r) * softplus
    a = jnp.exp(log_a)
    mult = jnp.sqrt(-jnp.tanh(log_a) * (a * a + 1.0))
    u = mult * (ig * xc)

    groups = tm // SUBLANES
    a3 = a.reshape(groups, SUBLANES, LRU_WIDTH)
    u3 = u.reshape(groups, SUBLANES, LRU_WIDTH)
    row = lax.broadcasted_iota(jnp.int32, (1, SUBLANES, LRU_WIDTH), 1)
    shift = 1
    while shift < SUBLANES:
        a_prev = pltpu.roll(a3, shift, axis=1)
        u_prev = pltpu.roll(u3, shift, axis=1)
        keep = row >= shift
        u3 = jnp.where(keep, a3 * u_prev + u3, u3)
        a3 = jnp.where(keep, a3 * a_prev, a3)
        shift *= 2
    a_scr[...] = a3.reshape(tm, LRU_WIDTH)
    u_scr[...] = u3.reshape(tm, LRU_WIDTH)

    def carry_group(g, h_prev):
        r0 = pl.multiple_of(g * SUBLANES, SUBLANES)
        hg = u_scr[pl.ds(r0, SUBLANES), :] + a_scr[pl.ds(r0, SUBLANES), :] * h_prev
        u_scr[pl.ds(r0, SUBLANES), :] = hg
        return hg[SUBLANES - 1:SUBLANES, :]

    seg_groups = seg_len // SUBLANES
    for j in range(nseg):
        h_init = hc_scr[0:1, :] if tiles_per_seq > 1 else h0_ref[j]
        h_last = lax.fori_loop(j * seg_groups, (j + 1) * seg_groups, carry_group, h_init)
        if tiles_per_seq > 1:
            hc_scr[0:1, :] = h_last
        hout_ref[j] = h_last
        convout_ref[j] = xpad_scr[j, tail, :]
        if tiles_per_seq > 1:
            xpad_scr[j, hist, :] = xpad_scr[j, tail, :]

    y = (u_scr[...] * gl_scr[...]).astype(BF16)
    lru_ref[...] = lru_ref[...] * jnp.dot(y, wbl_ref[...], preferred_element_type=F32)


def _proj_call(x2d, conv0, h0, p, *, seq_len, tm):
    n = x2d.shape[0]
    nb = n // seq_len
    seg_len = min(seq_len, tm)
    nseg = tm // seg_len
    tiles_per_seq = seq_len // seg_len
    assert n % tm == 0 and tm % seg_len == 0 and seq_len % seg_len == 0 and seg_len % SUBLANES == 0
    assert nseg == 1 or tiles_per_seq == 1

    if tiles_per_seq > 1:
        state_map = lambda i: (i // tiles_per_seq, 0, 0)
    else:
        state_map = lambda i: (i, 0, 0)
    row_map = lambda i: (i, 0)
    tok_f32 = jax.ShapeDtypeStruct((n, D_MODEL), F32)
    tok_bf16 = jax.ShapeDtypeStruct((n, D_MODEL), BF16)
    kernel = functools.partial(_proj_kernel, tm=tm, seg_len=seg_len, tiles_per_seq=tiles_per_seq)
    return pl.pallas_call(
        kernel,
        grid=(n // tm,),
        in_specs=[
            pl.BlockSpec((tm, D_MODEL), row_map),
            pl.BlockSpec((nseg, CONV_WIDTH - 1, LRU_WIDTH), state_map),
            pl.BlockSpec((nseg, 1, LRU_WIDTH), state_map),
            _const_spec((1, D_MODEL)),
            _const_spec(p["w_in"].shape),
            _const_spec(p["w_gate"].shape),
            _const_spec((2, LRU_WIDTH)),
            _const_spec((CONV_WIDTH, LRU_WIDTH)),
            _const_spec((1, LRU_WIDTH)),
            _const_spec((1, LRU_WIDTH)),
            _const_spec((LRU_WIDTH, D_MODEL)),
        ],
        out_specs=[
            pl.BlockSpec((tm, D_MODEL), row_map),
            pl.BlockSpec((tm, D_MODEL), row_map),
            pl.BlockSpec((tm, D_MODEL), row_map),
            pl.BlockSpec((tm, D_MODEL), row_map),
            pl.BlockSpec((tm, D_MODEL), row_map),
            pl.BlockSpec((tm, D_MODEL), row_map),
            pl.BlockSpec((tm, D_MODEL), row_map),
            pl.BlockSpec((nseg, CONV_WIDTH - 1, LRU_WIDTH), state_map),
            pl.BlockSpec((nseg, 1, LRU_WIDTH), state_map),
        ],
        out_shape=[tok_bf16, tok_f32, tok_f32, tok_bf16, tok_bf16, tok_f32, tok_f32,
                   jax.ShapeDtypeStruct((nb, CONV_WIDTH - 1, LRU_WIDTH), F32),
                   jax.ShapeDtypeStruct((nb, 1, LRU_WIDTH), F32)],
        scratch_shapes=[
            pltpu.VMEM((tm, D_MODEL), BF16),
            pltpu.VMEM((nseg, CONV_PAD + seg_len, LRU_WIDTH), F32),
            pltpu.VMEM((tm, LRU_WIDTH), F32),
            pltpu.VMEM((tm, LRU_WIDTH), F32),
            pltpu.VMEM((tm, LRU_WIDTH), F32),
            pltpu.VMEM((SUBLANES, LRU_WIDTH), F32),
        ],
        compiler_params=pltpu.CompilerParams(
            dimension_semantics=("arbitrary",), vmem_limit_bytes=VMEM_LIMIT_BYTES),
        name="proj_lru",
    )(x2d, conv0, h0, p["g_mix"], p["w_in"], p["w_gate"], p["b_gate"], p["conv_w"], p["conv_b"],
      p["lru_lambda"], p["w_bl"])


def _split_maps(q, tq, q2_scr):
    lane = lax.broadcasted_iota(jnp.int32, q.shape, 1)
    zero = jnp.zeros_like(q)
    q2_scr[0:tq, :] = jnp.where(lane < HEAD_DIM, q, zero)
    q2_scr[tq:2 * tq, :] = jnp.where(lane >= HEAD_DIM, q, zero)


def _lambda_full(lq_ref, lk_ref, lam_init):
    e0 = jnp.exp(jnp.sum(lq_ref[0:1, :] * lk_ref[0:1, :], axis=1, keepdims=True))
    e1 = jnp.exp(jnp.sum(lq_ref[1:2, :] * lk_ref[1:2, :], axis=1, keepdims=True))
    return e0 - e1 + lam_init


def _finish_head(acc, l_row, tq, lam, gain, lam_init):
    o1 = acc[0:tq] / l_row[0:tq]
    o2 = acc[tq:2 * tq] / l_row[tq:2 * tq]
    o = o1 - lam * o2
    o = o * lax.rsqrt(jnp.mean(o * o, axis=-1, keepdims=True) + EPS)
    return o * gain * (1.0 - lam_init)


def _online_step(s, v, m_scr, l_scr, acc_scr):
    nblk = s.shape[1] // LANES
    blocks = [s[:, c * LANES:(c + 1) * LANES] for c in range(nblk)]
    m_prev = m_scr[...]
    m_cur = blocks[0]
    for b in blocks[1:]:
        m_cur = jnp.maximum(m_cur, b)
    m_new = jnp.maximum(m_prev, jnp.max(m_cur, axis=1, keepdims=True))
    alpha = jnp.exp(m_prev - m_new)
    p_blocks = [jnp.exp(b - m_new) for b in blocks]
    p_sum = p_blocks[0]
    for pb in p_blocks[1:]:
        p_sum = p_sum + pb
    p = jnp.concatenate(p_blocks, axis=1).astype(BF16)
    l_scr[...] = alpha * l_scr[...] + p_sum
    acc_scr[...] = alpha * acc_scr[...] + jnp.dot(p, v, preferred_element_type=F32)
    m_scr[...] = m_new


def _attn_prompt_kernel(lq_ref, lk_ref, gain_ref, q_ref, k_ref, v_ref, o_ref,
                        q2_scr, m_scr, l_scr, acc_scr, *, tq, tk, lam_init):
    qi = pl.program_id(2)
    _split_maps(q_ref[...], tq, q2_scr)
    m_scr[...] = jnp.full(m_scr.shape, MASK_VALUE, F32)
    l_scr[...] = jnp.zeros(l_scr.shape, F32)
    acc_scr[...] = jnp.zeros(acc_scr.shape, F32)

    def scores(j):
        k0 = pl.multiple_of(j * tk, tk)
        k = k_ref[pl.ds(k0, tk), :]
        v = v_ref[pl.ds(k0, tk), :]
        s = lax.dot_general(q2_scr[...], k, (((1,), (1,)), ((), ())), preferred_element_type=F32)
        return s, v

    def full_tile(j, carry):
        s, v = scores(j)
        _online_step(s, v, m_scr, l_scr, acc_scr)
        return carry

    lax.fori_loop(0, qi * (tq // tk), full_tile, 0)

    q_chunk = (lax.broadcasted_iota(jnp.int32, (2 * tq, tk), 0) % tq) // CHUNK
    k_chunk = lax.broadcasted_iota(jnp.int32, (2 * tq, tk), 1) // CHUNK
    for d in range(tq // tk):
        s, v = scores(qi * (tq // tk) + d)
        visible = (k_chunk + d * (tk // CHUNK)) <= q_chunk
        _online_step(jnp.where(visible, s, MASK_VALUE), v, m_scr, l_scr, acc_scr)

    l_row = jnp.sum(l_scr[...], axis=1, keepdims=True)
    lam = _lambda_full(lq_ref, lk_ref, lam_init)
    o_ref[...] = _finish_head(acc_scr[...], l_row, tq, lam, gain_ref[...], lam_init).astype(o_ref.dtype)


def _attn_prompt_call(q, kb, vb, p, *, batch, seq_len, tq, tk, lam_init):
    assert seq_len % tq == 0 and tq % tk == 0 and tk % CHUNK == 0 and tk % LANES == 0
    nq = seq_len // tq
    kernel = functools.partial(_attn_prompt_kernel, tq=tq, tk=tk, lam_init=lam_init)
    return pl.pallas_call(
        kernel,
        grid=(batch, N_HEADS, nq),
        in_specs=[
            _const_spec((2, HEAD_DIM)),
            _const_spec((2, HEAD_DIM)),
            _const_spec((1, V_DIM)),
            pl.BlockSpec((tq, V_DIM), lambda b, h, i: (b * nq + i, h)),
            pl.BlockSpec((seq_len, V_DIM), lambda b, h, i: (b, h)),
            pl.BlockSpec((seq_len, V_DIM), lambda b, h, i: (b, h)),
        ],
        out_specs=pl.BlockSpec((tq, V_DIM), lambda b, h, i: (b * nq + i, h)),
        out_shape=jax.ShapeDtypeStruct(q.shape, BF16),
        scratch_shapes=[
            pltpu.VMEM((2 * tq, V_DIM), BF16),
            pltpu.VMEM((2 * tq, LANES), F32),
            pltpu.VMEM((2 * tq, LANES), F32),
            pltpu.VMEM((2 * tq, V_DIM), F32),
        ],
        compiler_params=pltpu.CompilerParams(
            dimension_semantics=("arbitrary", "arbitrary", "arbitrary"),
            vmem_limit_bytes=VMEM_LIMIT_BYTES),
        name="attn_prompt",
    )(p["lambda_q"], p["lambda_k"], p["head_gain"], q, kb, vb)


def _attn_sample_kernel(lq_ref, lk_ref, gain_ref, q_ref, kn_ref, vn_ref, kc_ref, vc_ref, o_ref,
                        q2_scr, *, tq, lam_init):
    lam = _lambda_full(lq_ref, lk_ref, lam_init)
    dims = (((1,), (1,)), ((), ()))
    for h in range(N_HEADS):
        cols = slice(h * V_DIM, (h + 1) * V_DIM)
        _split_maps(q_ref[:, cols], tq, q2_scr)
        q2 = q2_scr[...]
        s_c = lax.dot_general(q2, kc_ref[0, :, cols].astype(BF16), dims, preferred_element_type=F32)
        s_n = lax.dot_general(q2, kn_ref[:, cols], dims, preferred_element_type=F32)
        m = jnp.maximum(jnp.max(s_c, axis=1, keepdims=True), jnp.max(s_n, axis=1, keepdims=True))
        p_c = jnp.exp(s_c - m)
        p_n = jnp.exp(s_n - m)
        l_row = jnp.sum(p_c, axis=1, keepdims=True) + jnp.sum(p_n, axis=1, keepdims=True)
        acc = (jnp.dot(p_c.astype(BF16), vc_ref[0, :, cols].astype(BF16), preferred_element_type=F32)
               + jnp.dot(p_n.astype(BF16), vn_ref[:, cols], preferred_element_type=F32))
        o_ref[:, cols] = _finish_head(acc, l_row, tq, lam, gain_ref[...], lam_init).astype(o_ref.dtype)


def _attn_sample_call(q, kb, vb, cache_k, cache_v, p, *, batch, seq_len, lam_init):
    past = cache_k.shape[1]
    kernel = functools.partial(_attn_sample_kernel, tq=seq_len, lam_init=lam_init)
    return pl.pallas_call(
        kernel,
        grid=(batch,),
        in_specs=[
            _const_spec((2, HEAD_DIM)),
            _const_spec((2, HEAD_DIM)),
            _const_spec((1, V_DIM)),
            pl.BlockSpec((seq_len, D_MODEL), lambda b: (b, 0)),
            pl.BlockSpec((seq_len, D_MODEL), lambda b: (b, 0)),
            pl.BlockSpec((seq_len, D_MODEL), lambda b: (b, 0)),
            pl.BlockSpec((1, past, D_MODEL), lambda b: (b, 0, 0)),
            pl.BlockSpec((1, past, D_MODEL), lambda b: (b, 0, 0)),
        ],
        out_specs=pl.BlockSpec((seq_len, D_MODEL), lambda b: (b, 0)),
        out_shape=jax.ShapeDtypeStruct(q.shape, BF16),
        scratch_shapes=[pltpu.VMEM((2 * seq_len, V_DIM), BF16)],
        compiler_params=pltpu.CompilerParams(
            dimension_semantics=("arbitrary",), vmem_limit_bytes=VMEM_LIMIT_BYTES),
        name="attn_sample",
    )(p["lambda_q"], p["lambda_k"], p["head_gain"], q, kb, vb, cache_k, cache_v)


def _merge_kernel(x_ref, o_ref, ga_ref, lru_ref, wba_ref, wo_ref, gmlp_ref, wup_ref, wdown_ref,
                  gfin_ref, y_ref):
    attn = jnp.dot(o_ref[...], wba_ref[...], preferred_element_type=F32)
    merged = ga_ref[...] * attn + lru_ref[...]
    h = x_ref[...] + jnp.dot(merged.astype(BF16), wo_ref[...], preferred_element_type=F32)
    hn = _rms_norm_f32(h, gmlp_ref[...]).astype(BF16)
    mlp = None
    for c in range(D_FF // D_MODEL):
        cols = slice(c * D_MODEL, (c + 1) * D_MODEL)
        up = jnp.dot(hn, wup_ref[:, cols], preferred_element_type=F32)
        act = jnp.square(jnp.maximum(up, 0.0)).astype(BF16)
        part = jnp.dot(act, wdown_ref[cols, :], preferred_element_type=F32)
        mlp = part if mlp is None else mlp + part
    y_ref[...] = _rms_norm_f32(h + mlp, gfin_ref[...])


def _merge_call(x2d, o_attn, ga, lru, p, *, tm):
    n = x2d.shape[0]
    assert n % tm == 0
    row_map = lambda i: (i, 0)
    return pl.pallas_call(
        _merge_kernel,
        grid=(n // tm,),
        in_specs=[
            pl.BlockSpec((tm, D_MODEL), row_map),
            pl.BlockSpec((tm, D_MODEL), row_map),
            pl.BlockSpec((tm, D_MODEL), row_map),
            pl.BlockSpec((tm, D_MODEL), row_map),
            _const_spec((D_MODEL, D_MODEL)),
            _const_spec((D_MODEL, D_MODEL)),
            _const_spec((1, D_MODEL)),
            _const_spec((D_MODEL, D_FF)),
            _const_spec((D_FF, D_MODEL)),
            _const_spec((1, D_MODEL)),
        ],
        out_specs=pl.BlockSpec((tm, D_MODEL), row_map),
        out_shape=jax.ShapeDtypeStruct((n, D_MODEL), F32),
        compiler_params=pltpu.CompilerParams(
            dimension_semantics=("arbitrary",), vmem_limit_bytes=VMEM_LIMIT_BYTES),
        name="merge_mlp",
    )(x2d, o_attn, ga, lru, p["w_ba"], p["w_o"], p["g_mlp"], p["w_up"], p["w_down"], p["g_final"])


def _block_diag_groups(w):
    w4 = w.reshape(N_GATE_GROUPS, GATE_GROUP, LRU_BLOCK, LRU_BLOCK)
    eye = jnp.eye(GATE_GROUP, dtype=w.dtype)
    return jnp.einsum("gaij,ab->gaibj", w4, eye).reshape(N_GATE_GROUPS, MXU_DEPTH, MXU_DEPTH)


def _layer_params(l, norm_mix, norm_mlp, norm_final, w_in, lambda_q, lambda_k, head_gain, conv_w,
                  conv_b, w_rgate, b_rgate, w_igate, b_igate, lru_lambda, w_branch_attn,
                  w_branch_lru, w_out, w_mlp_up, w_mlp_down):
    return {
        "g_mix": norm_mix[l][None, :],
        "g_mlp": norm_mlp[l][None, :],
        "g_final": norm_final[None, :],
        "w_in": w_in[l].astype(BF16),
        "lambda_q": lambda_q[l],
        "lambda_k": lambda_k[l],
        "head_gain": head_gain[l][None, :],
        "conv_w": conv_w[l],
        "conv_b": conv_b[l][None, :],
        "w_gate": jnp.stack([_block_diag_groups(w_rgate[l]), _block_diag_groups(w_igate[l])]).astype(BF16),
        "b_gate": jnp.stack([b_rgate[l], b_igate[l]]),
        "lru_lambda": lru_lambda[l][None, :],
        "w_bl": w_branch_lru[l].astype(BF16),
        "w_ba": w_branch_attn[l].astype(BF16),
        "w_o": w_out[l].astype(BF16),
        "w_up": w_mlp_up[l].astype(BF16),
        "w_down": w_mlp_down[l].astype(BF16),
    }


PROJ_TM = 256
MERGE_TM = 256
ATTN_TQ = 512
ATTN_TK = 512


def kernel(x_prompt, x_sample, cache_k, cache_v, state_conv, state_lru, norm_mix, norm_mlp, norm_final, w_in, lambda_q, lambda_k, head_gain, conv_w, conv_b, w_rgate, b_rgate, w_igate, b_igate, lru_lambda, w_branch_attn, w_branch_lru, w_out, w_mlp_up, w_mlp_down):
    bp, sp, _ = x_prompt.shape
    bs, ss, _ = x_sample.shape
    depth = w_in.shape[0]
    assert depth == 1, "the final norm is fused into the last (only) layer's merge kernel"
    past = cache_k.shape[2]
    assert past % CHUNK == 0 and ss <= CHUNK

    l = 0
    lam_init = 0.8 - 0.6 * math.exp(-0.3 * l)
    p = _layer_params(l, norm_mix, norm_mlp, norm_final, w_in, lambda_q, lambda_k, head_gain, conv_w,
                      conv_b, w_rgate, b_rgate, w_igate, b_igate, lru_lambda, w_branch_attn,
                      w_branch_lru, w_out, w_mlp_up, w_mlp_down)

    xp2 = x_prompt.reshape(bp * sp, D_MODEL)
    conv0_p = jnp.zeros((bp, CONV_WIDTH - 1, LRU_WIDTH), F32)
    h0_p = jnp.zeros((bp, 1, LRU_WIDTH), F32)
    q_p, k_p, v_p, kb_p, vb_p, ga_p, lru_p, conv_p, hl_p = _proj_call(
        xp2, conv0_p, h0_p, p, seq_len=sp, tm=PROJ_TM)
    o_p = _attn_prompt_call(q_p, kb_p, vb_p, p, batch=bp, seq_len=sp, tq=ATTN_TQ, tk=ATTN_TK,
                            lam_init=lam_init)
    y_p = _merge_call(xp2, o_p, ga_p, lru_p, p, tm=MERGE_TM)

    xs2 = x_sample.reshape(bs * ss, D_MODEL)
    q_s, k_s, v_s, kb_s, vb_s, ga_s, lru_s, conv_s, hl_s = _proj_call(
        xs2, state_conv[l], state_lru[l][:, None, :], p, seq_len=ss, tm=PROJ_TM)
    ck = cache_k[l].reshape(bs, past, D_MODEL)
    cv = cache_v[l].reshape(bs, past, D_MODEL)
    o_s = _attn_sample_call(q_s, kb_s, vb_s, ck, cv, p, batch=bs, seq_len=ss, lam_init=lam_init)
    y_s = _merge_call(xs2, o_s, ga_s, lru_s, p, tm=MERGE_TM)

    return (
        y_p.reshape(bp, sp, D_MODEL),
        y_s.reshape(bs, ss, D_MODEL),
        k_p.reshape(1, bp, sp, N_HEADS, 2, HEAD_DIM),
        v_p.reshape(1, bp, sp, N_HEADS, V_DIM),
        conv_p[None],
        hl_p.reshape(1, bp, LRU_WIDTH),
        k_s.reshape(1, bs, ss, N_HEADS, 2, HEAD_DIM),
        v_s.reshape(1, bs, ss, N_HEADS, V_DIM),
        conv_s[None],
        hl_s.reshape(1, bs, LRU_WIDTH),
    )
```

```python
import functools
import math

import jax
import jax.numpy as jnp
import numpy as np
from jax import lax
from jax.experimental import pallas as pl
from jax.experimental.pallas import tpu as pltpu

D_MODEL = 1024
N_HEADS = 8
HEAD_DIM = 64
V_DIM = 2 * HEAD_DIM
CHUNK = 64
LRU_WIDTH = D_MODEL
LRU_BLOCKS = 16
LRU_BLOCK = LRU_WIDTH // LRU_BLOCKS
CONV_WIDTH = 4
LRU_C = 8.0
D_FF = 4 * D_MODEL
EPS = 1e-6

SUBLANES = 8
LANES = 128
MXU_DEPTH = 256
GATE_GROUP = MXU_DEPTH // LRU_BLOCK
N_GATE_GROUPS = LRU_BLOCKS // GATE_GROUP
CONV_PAD = SUBLANES
MASK_VALUE = -0.7 * float(np.finfo(np.float32).max)
VMEM_LIMIT_BYTES = 56 * 1024 * 1024

F32 = jnp.float32
BF16 = jnp.bfloat16
NT_DIMS = (((1,), (1,)), ((), ()))


def _rms_norm_f32(x, g):
    return x * lax.rsqrt(jnp.mean(x * x, axis=-1, keepdims=True) + EPS) * g


def _const_spec(shape):
    zeros = (0,) * len(shape)
    return pl.BlockSpec(shape, lambda *_: zeros, pipeline_mode=pl.Buffered(1))


def _store_head_rows(v_ref, val, rows):
    for h in range(N_HEADS):
        v_ref[pl.ds(h, rows, stride=N_HEADS), :] = val[:, h * V_DIM:(h + 1) * V_DIM]


def _proj_kernel(x_ref, conv0_ref, h0_ref, gmix_ref, win_ref, wkt_ref, wg_ref, bg_ref, convw_ref,
                 convb_ref, lam_ref, wbl_ref,
                 q_ref, k_ref, v_ref, kb_ref, vb_ref, ga_ref, lru_ref, convout_ref, hout_ref,
                 xn_scr, xpad_scr, gl_scr, a_scr, u_scr, hc_scr,
                 *, tm, seg_len, tiles_per_seq, keys_position_minor):
    nseg = tm // seg_len
    i = pl.program_id(0)

    x = x_ref[...]
    xn_scr[...] = _rms_norm_f32(x, gmix_ref[...]).astype(BF16)

    def proj(c):
        return jnp.dot(xn_scr[...], win_ref[:, c * D_MODEL:(c + 1) * D_MODEL],
                       preferred_element_type=F32)

    q_ref[...] = (proj(0) * (HEAD_DIM ** -0.5)).astype(BF16)
    if keys_position_minor:
        kt = lax.dot_general(wkt_ref[...], xn_scr[...], NT_DIMS, preferred_element_type=F32)
        k_ref[0] = kt
        kb_ref[0] = kt.astype(BF16)
    else:
        kk = proj(1)
        k_ref[...] = kk
        kb_ref[...] = kk.astype(BF16)
    vv = proj(2)
    _store_head_rows(v_ref, vv, tm)
    vb_ref[...] = vv.astype(BF16)
    xl = proj(3)
    for j in range(nseg):
        xpad_scr[j, CONV_PAD:CONV_PAD + seg_len, :] = xl[j * seg_len:(j + 1) * seg_len]
    gl_scr[...] = jax.nn.gelu(proj(4))
    ga_ref[...] = jax.nn.sigmoid(proj(5))
    lru_ref[...] = jax.nn.sigmoid(proj(6))

    def stream(j):
        return i // tiles_per_seq if tiles_per_seq > 1 else i * nseg + j

    hist0 = CONV_PAD - (CONV_WIDTH - 1)
    tail0 = CONV_PAD + seg_len - (CONV_WIDTH - 1)

    def load_initial_state(j):
        for t in range(CONV_WIDTH - 1):
            xpad_scr[j, hist0 + t:hist0 + t + 1, :] = conv0_ref[t, pl.ds(stream(j), 1), :]

    if tiles_per_seq > 1:
        @pl.when(i % tiles_per_seq == 0)
        def _():
            load_initial_state(0)
            hc_scr[0:1, :] = h0_ref[pl.ds(stream(0), 1), :]
    else:
        for j in range(nseg):
            load_initial_state(j)

    convw = convw_ref[...]
    xc_parts = []
    for j in range(nseg):
        acc = convb_ref[...] + convw[0:1, :] * xpad_scr[j, pl.ds(hist0, seg_len), :]
        for t in range(1, CONV_WIDTH):
            acc = acc + convw[t:t + 1, :] * xpad_scr[j, pl.ds(hist0 + t, seg_len), :]
        xc_parts.append(acc)
    xc = xc_parts[0] if nseg == 1 else jnp.concatenate(xc_parts, axis=0)

    xcb = xc.astype(BF16)

    def gate(which):
        cols = [jnp.dot(xcb[:, g * MXU_DEPTH:(g + 1) * MXU_DEPTH], wg_ref[which, g],
                        preferred_element_type=F32) for g in range(N_GATE_GROUPS)]
        return jax.nn.sigmoid(jnp.concatenate(cols, axis=1) + bg_ref[which:which + 1, :])

    r = gate(0)
    ig = gate(1)
    z = -lam_ref[...]
    softplus = jnp.maximum(z, 0.0) + jnp.log1p(jnp.exp(-jnp.abs(z)))
    log_a = (-LRU_C * r) * softplus
    a = jnp.exp(log_a)
    mult = jnp.sqrt(-jnp.tanh(log_a) * (a * a + 1.0))
    u = mult * (ig * xc)

    groups = tm // SUBLANES
    a3 = a.reshape(groups, SUBLANES, LRU_WIDTH)
    u3 = u.reshape(groups, SUBLANES, LRU_WIDTH)
    row = lax.broadcasted_iota(jnp.int32, (1, SUBLANES, LRU_WIDTH), 1)
    shift = 1
    while shift < SUBLANES:
        a_prev = pltpu.roll(a3, shift, axis=1)
        u_prev = pltpu.roll(u3, shift, axis=1)
        keep = row >= shift
        u3 = jnp.where(keep, a3 * u_prev + u3, u3)
        a3 = jnp.where(keep, a3 * a_prev, a3)
        shift *= 2
    a_scr[...] = a3.reshape(tm, LRU_WIDTH)
    u_scr[...] = u3.reshape(tm, LRU_WIDTH)

    def carry_group(g, h_prev):
        r0 = pl.multiple_of(g * SUBLANES, SUBLANES)
        hg = u_scr[pl.ds(r0, SUBLANES), :] + a_scr[pl.ds(r0, SUBLANES), :] * h_prev
        u_scr[pl.ds(r0, SUBLANES), :] = hg
        return hg[SUBLANES - 1:SUBLANES, :]

    seg_groups = seg_len // SUBLANES
    for j in range(nseg):
        b = stream(j)
        h_init = hc_scr[0:1, :] if tiles_per_seq > 1 else h0_ref[pl.ds(b, 1), :]
        h_last = lax.fori_loop(j * seg_groups, (j + 1) * seg_groups, carry_group, h_init)
        hout_ref[pl.ds(b, 1), :] = h_last
        for t in range(CONV_WIDTH - 1):
            convout_ref[t, pl.ds(b, 1), :] = xpad_scr[j, tail0 + t:tail0 + t + 1, :]
        if tiles_per_seq > 1:
            hc_scr[0:1, :] = h_last
            xpad_scr[j, hist0:CONV_PAD, :] = xpad_scr[j, tail0:tail0 + CONV_WIDTH - 1, :]

    y = (u_scr[...] * gl_scr[...]).astype(BF16)
    lru_ref[...] = lru_ref[...] * jnp.dot(y, wbl_ref[...], preferred_element_type=F32)


def _proj_call(x2d, conv0, h0, p, *, seq_len, tm, keys_position_minor):
    n = x2d.shape[0]
    nb = n // seq_len
    seg_len = min(seq_len, tm)
    nseg = tm // seg_len
    tiles_per_seq = seq_len // seg_len
    assert n % tm == 0 and tm % seg_len == 0 and seq_len % seg_len == 0 and seg_len % SUBLANES == 0
    assert nseg == 1 or tiles_per_seq == 1

    row_map = lambda i: (i, 0)
    tok_f32 = jax.ShapeDtypeStruct((n, D_MODEL), F32)
    tok_bf16 = jax.ShapeDtypeStruct((n, D_MODEL), BF16)
    if keys_position_minor:
        kt_map = lambda i: (i // tiles_per_seq, 0, i % tiles_per_seq)
        k_specs = [pl.BlockSpec((1, D_MODEL, tm), kt_map), pl.BlockSpec((1, D_MODEL, tm), kt_map)]
        k_shapes = [jax.ShapeDtypeStruct((nb, D_MODEL, seq_len), F32),
                    jax.ShapeDtypeStruct((nb, D_MODEL, seq_len), BF16)]
    else:
        k_specs = [pl.BlockSpec((tm, D_MODEL), row_map), pl.BlockSpec((tm, D_MODEL), row_map)]
        k_shapes = [tok_f32, tok_bf16]
    kernel = functools.partial(_proj_kernel, tm=tm, seg_len=seg_len, tiles_per_seq=tiles_per_seq,
                               keys_position_minor=keys_position_minor)
    return pl.pallas_call(
        kernel,
        grid=(n // tm,),
        in_specs=[
            pl.BlockSpec((tm, D_MODEL), row_map),
            _const_spec(conv0.shape),
            _const_spec(h0.shape),
            _const_spec((1, D_MODEL)),
            _const_spec(p["w_in"].shape),
            _const_spec(p["w_kt"].shape),
            _const_spec(p["w_gate"].shape),
            _const_spec((2, LRU_WIDTH)),
            _const_spec((CONV_WIDTH, LRU_WIDTH)),
            _const_spec((1, LRU_WIDTH)),
            _const_spec((1, LRU_WIDTH)),
            _const_spec((LRU_WIDTH, D_MODEL)),
        ],
        out_specs=[
            pl.BlockSpec((tm, D_MODEL), row_map),
            k_specs[0],
            pl.BlockSpec((tm * N_HEADS, V_DIM), row_map),
            k_specs[1],
            pl.BlockSpec((tm, D_MODEL), row_map),
            pl.BlockSpec((tm, D_MODEL), row_map),
            pl.BlockSpec((tm, D_MODEL), row_map),
            pl.BlockSpec(conv0.shape, lambda i: (0, 0, 0)),
            pl.BlockSpec(h0.shape, lambda i: (0, 0)),
        ],
        out_shape=[tok_bf16, k_shapes[0], jax.ShapeDtypeStruct((n * N_HEADS, V_DIM), F32), k_shapes[1],
                   tok_bf16, tok_f32, tok_f32,
                   jax.ShapeDtypeStruct(conv0.shape, F32), jax.ShapeDtypeStruct(h0.shape, F32)],
        scratch_shapes=[
            pltpu.VMEM((tm, D_MODEL), BF16),
            pltpu.VMEM((nseg, CONV_PAD + seg_len, LRU_WIDTH), F32),
            pltpu.VMEM((tm, LRU_WIDTH), F32),
            pltpu.VMEM((tm, LRU_WIDTH), F32),
            pltpu.VMEM((tm, LRU_WIDTH), F32),
            pltpu.VMEM((SUBLANES, LRU_WIDTH), F32),
        ],
        compiler_params=pltpu.CompilerParams(
            dimension_semantics=("arbitrary",), vmem_limit_bytes=VMEM_LIMIT_BYTES),
        name="proj_lru",
    )(x2d, conv0, h0, p["g_mix"], p["w_in"], p["w_kt"], p["w_gate"], p["b_gate"], p["conv_w"],
      p["conv_b"], p["lru_lambda"], p["w_bl"])


def _split_maps(q, tq, q2_scr):
    lane = lax.broadcasted_iota(jnp.int32, q.shape, 1)
    zero = jnp.zeros_like(q)
    q2_scr[0:tq, :] = jnp.where(lane < HEAD_DIM, q, zero)
    q2_scr[tq:2 * tq, :] = jnp.where(lane >= HEAD_DIM, q, zero)


def _lambda_full(lq_ref, lk_ref, lam_init):
    e0 = jnp.exp(jnp.sum(lq_ref[0:1, :] * lk_ref[0:1, :], axis=1, keepdims=True))
    e1 = jnp.exp(jnp.sum(lq_ref[1:2, :] * lk_ref[1:2, :], axis=1, keepdims=True))
    return e0 - e1 + lam_init


def _finish_head(acc, l_row, tq, lam, gain, lam_init):
    o1 = acc[0:tq] / l_row[0:tq]
    o2 = acc[tq:2 * tq] / l_row[tq:2 * tq]
    o = o1 - lam * o2
    o = o * lax.rsqrt(jnp.mean(o * o, axis=-1, keepdims=True) + EPS)
    return o * gain * (1.0 - lam_init)


def _online_step(s, v, m_scr, l_scr, acc_scr):
    nblk = s.shape[1] // LANES
    blocks = [s[:, c * LANES:(c + 1) * LANES] for c in range(nblk)]
    m_prev = m_scr[...]
    m_cur = blocks[0]
    for b in blocks[1:]:
        m_cur = jnp.maximum(m_cur, b)
    m_new = jnp.maximum(m_prev, jnp.max(m_cur, axis=1, keepdims=True))
    alpha = jnp.exp(m_prev - m_new)
    p_blocks = [jnp.exp(b - m_new) for b in blocks]
    p_sum = p_blocks[0]
    for pb in p_blocks[1:]:
        p_sum = p_sum + pb
    p = jnp.concatenate(p_blocks, axis=1).astype(BF16)
    l_scr[...] = alpha * l_scr[...] + p_sum
    acc_scr[...] = alpha * acc_scr[...] + jnp.dot(p, v, preferred_element_type=F32)
    m_scr[...] = m_new


def _attn_prompt_kernel(lq_ref, lk_ref, gain_ref, q_ref, kt_ref, v_ref, o_ref,
                        q2_scr, m_scr, l_scr, acc_scr, *, tq, tk, lam_init):
    qi = pl.program_id(2)
    _split_maps(q_ref[...], tq, q2_scr)
    m_scr[...] = jnp.full(m_scr.shape, MASK_VALUE, F32)
    l_scr[...] = jnp.zeros(l_scr.shape, F32)
    acc_scr[...] = jnp.zeros(acc_scr.shape, F32)

    def scores(j):
        k0 = pl.multiple_of(j * tk, tk)
        s = jnp.dot(q2_scr[...], kt_ref[0, :, pl.ds(k0, tk)], preferred_element_type=F32)
        return s, v_ref[pl.ds(k0, tk), :]

    def full_tile(j, carry):
        s, v = scores(j)
        _online_step(s, v, m_scr, l_scr, acc_scr)
        return carry

    lax.fori_loop(0, qi * (tq // tk), full_tile, 0)

    q_chunk = (lax.broadcasted_iota(jnp.int32, (2 * tq, tk), 0) % tq) // CHUNK
    k_chunk = lax.broadcasted_iota(jnp.int32, (2 * tq, tk), 1) // CHUNK
    for d in range(tq // tk):
        s, v = scores(qi * (tq // tk) + d)
        visible = (k_chunk + d * (tk // CHUNK)) <= q_chunk
        _online_step(jnp.where(visible, s, MASK_VALUE), v, m_scr, l_scr, acc_scr)

    l_row = jnp.sum(l_scr[...], axis=1, keepdims=True)
    lam = _lambda_full(lq_ref, lk_ref, lam_init)
    o_ref[...] = _finish_head(acc_scr[...], l_row, tq, lam, gain_ref[...], lam_init).astype(o_ref.dtype)


def _attn_prompt_call(q, kbt, vb, p, *, batch, seq_len, tq, tk, lam_init):
    assert seq_len % tq == 0 and tq % tk == 0 and tk % CHUNK == 0 and tk % LANES == 0
    nq = seq_len // tq
    kernel = functools.partial(_attn_prompt_kernel, tq=tq, tk=tk, lam_init=lam_init)
    return pl.pallas_call(
        kernel,
        grid=(batch, N_HEADS, nq),
        in_specs=[
            _const_spec((2, HEAD_DIM)),
            _const_spec((2, HEAD_DIM)),
            _const_spec((1, V_DIM)),
            pl.BlockSpec((tq, V_DIM), lambda b, h, i: (b * nq + i, h)),
            pl.BlockSpec((1, V_DIM, seq_len), lambda b, h, i: (b, h, 0)),
            pl.BlockSpec((seq_len, V_DIM), lambda b, h, i: (b, h)),
        ],
        out_specs=pl.BlockSpec((tq, V_DIM), lambda b, h, i: (b * nq + i, h)),
        out_shape=jax.ShapeDtypeStruct(q.shape, BF16),
        scratch_shapes=[
            pltpu.VMEM((2 * tq, V_DIM), BF16),
            pltpu.VMEM((2 * tq, LANES), F32),
            pltpu.VMEM((2 * tq, LANES), F32),
            pltpu.VMEM((2 * tq, V_DIM), F32),
        ],
        compiler_params=pltpu.CompilerParams(
            dimension_semantics=("arbitrary", "arbitrary", "arbitrary"),
            vmem_limit_bytes=VMEM_LIMIT_BYTES),
        name="attn_prompt",
    )(p["lambda_q"], p["lambda_k"], p["head_gain"], q, kbt, vb)


def _attn_sample_kernel(lq_ref, lk_ref, gain_ref, q_ref, kn_ref, vn_ref, kct_ref, vc_ref, o_ref,
                        q2_scr, *, tq, past, lam_init):
    lam = _lambda_full(lq_ref, lk_ref, lam_init)
    for h in range(N_HEADS):
        cols = slice(h * V_DIM, (h + 1) * V_DIM)
        _split_maps(q_ref[:, cols], tq, q2_scr)
        q2 = q2_scr[...]
        s_c = jnp.dot(q2, kct_ref[0, h].astype(BF16), preferred_element_type=F32)
        s_n = lax.dot_general(q2, kn_ref[:, cols], NT_DIMS, preferred_element_type=F32)
        m = jnp.maximum(jnp.max(s_c, axis=1, keepdims=True), jnp.max(s_n, axis=1, keepdims=True))
        p_c = jnp.exp(s_c - m)
        p_n = jnp.exp(s_n - m)
        l_row = jnp.sum(p_c, axis=1, keepdims=True) + jnp.sum(p_n, axis=1, keepdims=True)
        v_c = vc_ref[0, pl.ds(h, past, stride=N_HEADS), :].astype(BF16)
        acc = (jnp.dot(p_c.astype(BF16), v_c, preferred_element_type=F32)
               + jnp.dot(p_n.astype(BF16), vn_ref[:, cols], preferred_element_type=F32))
        o_ref[:, cols] = _finish_head(acc, l_row, tq, lam, gain_ref[...], lam_init).astype(o_ref.dtype)


def _attn_sample_call(q, kb, vb, cache_kt, cache_v, p, *, batch, seq_len, lam_init):
    past = cache_kt.shape[3]
    kernel = functools.partial(_attn_sample_kernel, tq=seq_len, past=past, lam_init=lam_init)
    return pl.pallas_call(
        kernel,
        grid=(batch,),
        in_specs=[
            _const_spec((2, HEAD_DIM)),
            _const_spec((2, HEAD_DIM)),
            _const_spec((1, V_DIM)),
            pl.BlockSpec((seq_len, D_MODEL), lambda b: (b, 0)),
            pl.BlockSpec((seq_len, D_MODEL), lambda b: (b, 0)),
            pl.BlockSpec((seq_len, D_MODEL), lambda b: (b, 0)),
            pl.BlockSpec((1, N_HEADS, V_DIM, past), lambda b: (b, 0, 0, 0)),
            pl.BlockSpec((1, past * N_HEADS, V_DIM), lambda b: (b, 0, 0)),
        ],
        out_specs=pl.BlockSpec((seq_len, D_MODEL), lambda b: (b, 0)),
        out_shape=jax.ShapeDtypeStruct(q.shape, BF16),
        scratch_shapes=[pltpu.VMEM((2 * seq_len, V_DIM), BF16)],
        compiler_params=pltpu.CompilerParams(
            dimension_semantics=("arbitrary",), vmem_limit_bytes=VMEM_LIMIT_BYTES),
        name="attn_sample",
    )(p["lambda_q"], p["lambda_k"], p["head_gain"], q, kb, vb, cache_kt, cache_v)


def _merge_kernel(x_ref, o_ref, ga_ref, lru_ref, wba_ref, wo_ref, gmlp_ref, wup_ref, wdown_ref,
                  gfin_ref, y_ref):
    attn = jnp.dot(o_ref[...], wba_ref[...], preferred_element_type=F32)
    merged = ga_ref[...] * attn + lru_ref[...]
    h = x_ref[...] + jnp.dot(merged.astype(BF16), wo_ref[...], preferred_element_type=F32)
    hn = _rms_norm_f32(h, gmlp_ref[...]).astype(BF16)
    mlp = None
    for c in range(D_FF // D_MODEL):
        cols = slice(c * D_MODEL, (c + 1) * D_MODEL)
        up = jnp.dot(hn, wup_ref[:, cols], preferred_element_type=F32)
        act = jnp.square(jnp.maximum(up, 0.0)).astype(BF16)
        part = jnp.dot(act, wdown_ref[cols, :], preferred_element_type=F32)
        mlp = part if mlp is None else mlp + part
    y_ref[...] = _rms_norm_f32(h + mlp, gfin_ref[...])


def _merge_call(x2d, o_attn, ga, lru, p, *, tm):
    n = x2d.shape[0]
    assert n % tm == 0
    row_map = lambda i: (i, 0)
    return pl.pallas_call(
        _merge_kernel,
        grid=(n // tm,),
        in_specs=[
            pl.BlockSpec((tm, D_MODEL), row_map),
            pl.BlockSpec((tm, D_MODEL), row_map),
            pl.BlockSpec((tm, D_MODEL), row_map),
            pl.BlockSpec((tm, D_MODEL), row_map),
            _const_spec((D_MODEL, D_MODEL)),
            _const_spec((D_MODEL, D_MODEL)),
            _const_spec((1, D_MODEL)),
            _const_spec((D_MODEL, D_FF)),
            _const_spec((D_FF, D_MODEL)),
            _const_spec((1, D_MODEL)),
        ],
        out_specs=pl.BlockSpec((tm, D_MODEL), row_map),
        out_shape=jax.ShapeDtypeStruct((n, D_MODEL), F32),
        compiler_params=pltpu.CompilerParams(
            dimension_semantics=("arbitrary",), vmem_limit_bytes=VMEM_LIMIT_BYTES),
        name="merge_mlp",
    )(x2d, o_attn, ga, lru, p["w_ba"], p["w_o"], p["g_mlp"], p["w_up"], p["w_down"], p["g_final"])


def _block_diag_groups(w):
    w4 = w.reshape(N_GATE_GROUPS, GATE_GROUP, LRU_BLOCK, LRU_BLOCK)
    eye = jnp.eye(GATE_GROUP, dtype=w.dtype)
    return jnp.einsum("gaij,ab->gaibj", w4, eye).reshape(N_GATE_GROUPS, MXU_DEPTH, MXU_DEPTH)


def _layer_params(l, norm_mix, norm_mlp, norm_final, w_in, lambda_q, lambda_k, head_gain, conv_w,
                  conv_b, w_rgate, b_rgate, w_igate, b_igate, lru_lambda, w_branch_attn,
                  w_branch_lru, w_out, w_mlp_up, w_mlp_down):
    w_in_b = w_in[l].astype(BF16)
    return {
        "g_mix": norm_mix[l][None, :],
        "g_mlp": norm_mlp[l][None, :],
        "g_final": norm_final[None, :],
        "w_in": w_in_b,
        "w_kt": w_in_b[:, D_MODEL:2 * D_MODEL].T,
        "lambda_q": lambda_q[l],
        "lambda_k": lambda_k[l],
        "head_gain": head_gain[l][None, :],
        "conv_w": conv_w[l],
        "conv_b": conv_b[l][None, :],
        "w_gate": jnp.stack([_block_diag_groups(w_rgate[l]), _block_diag_groups(w_igate[l])]).astype(BF16),
        "b_gate": jnp.stack([b_rgate[l], b_igate[l]]),
        "lru_lambda": lru_lambda[l][None, :],
        "w_bl": w_branch_lru[l].astype(BF16),
        "w_ba": w_branch_attn[l].astype(BF16),
        "w_o": w_out[l].astype(BF16),
        "w_up": w_mlp_up[l].astype(BF16),
        "w_down": w_mlp_down[l].astype(BF16),
    }


PROJ_TM = 256
MERGE_TM = 256
ATTN_TQ = 512
ATTN_TK = 512


def kernel(x_prompt, x_sample, cache_k, cache_v, state_conv, state_lru, norm_mix, norm_mlp, norm_final, w_in, lambda_q, lambda_k, head_gain, conv_w, conv_b, w_rgate, b_rgate, w_igate, b_igate, lru_lambda, w_branch_attn, w_branch_lru, w_out, w_mlp_up, w_mlp_down):
    bp, sp, _ = x_prompt.shape
    bs, ss, _ = x_sample.shape
    depth = w_in.shape[0]
    assert depth == 1, "the final norm is fused into the last (only) layer's merge kernel"
    past = cache_k.shape[2]
    assert past % CHUNK == 0 and ss <= CHUNK

    l = 0
    lam_init = 0.8 - 0.6 * math.exp(-0.3 * l)
    p = _layer_params(l, norm_mix, norm_mlp, norm_final, w_in, lambda_q, lambda_k, head_gain, conv_w,
                      conv_b, w_rgate, b_rgate, w_igate, b_igate, lru_lambda, w_branch_attn,
                      w_branch_lru, w_out, w_mlp_up, w_mlp_down)

    xp2 = x_prompt.reshape(bp * sp, D_MODEL)
    conv0_p = jnp.zeros((CONV_WIDTH - 1, bp, LRU_WIDTH), F32)
    h0_p = jnp.zeros((bp, LRU_WIDTH), F32)
    q_p, kt_p, v_p, kbt_p, vb_p, ga_p, lru_p, conv_p, hl_p = _proj_call(
        xp2, conv0_p, h0_p, p, seq_len=sp, tm=PROJ_TM, keys_position_minor=True)
    o_p = _attn_prompt_call(q_p, kbt_p, vb_p, p, batch=bp, seq_len=sp, tq=ATTN_TQ, tk=ATTN_TK,
                            lam_init=lam_init)
    y_p = _merge_call(xp2, o_p, ga_p, lru_p, p, tm=MERGE_TM)

    xs2 = x_sample.reshape(bs * ss, D_MODEL)
    conv0_s = jnp.transpose(state_conv[l], (1, 0, 2))
    q_s, k_s, v_s, kb_s, vb_s, ga_s, lru_s, conv_s, hl_s = _proj_call(
        xs2, conv0_s, state_lru[l], p, seq_len=ss, tm=PROJ_TM, keys_position_minor=False)
    ckt = jnp.transpose(cache_k[l], (0, 2, 3, 4, 1)).reshape(bs, N_HEADS, V_DIM, past)
    cv = cache_v[l].reshape(bs, past * N_HEADS, V_DIM)
    o_s = _attn_sample_call(q_s, kb_s, vb_s, ckt, cv, p, batch=bs, seq_len=ss, lam_init=lam_init)
    y_s = _merge_call(xs2, o_s, ga_s, lru_s, p, tm=MERGE_TM)

    k_prompt = jnp.transpose(kt_p.reshape(bp, N_HEADS, 2, HEAD_DIM, sp), (0, 4, 1, 2, 3))
    return (
        y_p.reshape(bp, sp, D_MODEL),
        y_s.reshape(bs, ss, D_MODEL),
        k_prompt[None],
        v_p.reshape(1, bp, sp, N_HEADS, V_DIM),
        jnp.transpose(conv_p, (1, 0, 2))[None],
        hl_p[None],
        k_s.reshape(1, bs, ss, N_HEADS, 2, HEAD_DIM),
        v_s.reshape(1, bs, ss, N_HEADS, V_DIM),
        jnp.transpose(conv_s, (1, 0, 2))[None],
        hl_s[None],
    )
```

```python
import functools
import math

import jax
import jax.numpy as jnp
import numpy as np
from jax import lax
from jax.experimental import pallas as pl
from jax.experimental.pallas import tpu as pltpu

D_MODEL = 1024
N_HEADS = 8
HEAD_DIM = 64
V_DIM = 2 * HEAD_DIM
CHUNK = 64
LRU_WIDTH = D_MODEL
LRU_BLOCKS = 16
LRU_BLOCK = LRU_WIDTH // LRU_BLOCKS
CONV_WIDTH = 4
LRU_C = 8.0
D_FF = 4 * D_MODEL
EPS = 1e-6

SUBLANES = 8
LANES = 128
MXU_DEPTH = 256
GATE_GROUP = MXU_DEPTH // LRU_BLOCK
N_GATE_GROUPS = LRU_BLOCKS // GATE_GROUP
CONV_PAD = SUBLANES
MASK_VALUE = -0.7 * float(np.finfo(np.float32).max)
Q_SCALE_LOG2 = HEAD_DIM ** -0.5 * math.log2(math.e)
VMEM_LIMIT_BYTES = 56 * 1024 * 1024

F32 = jnp.float32
BF16 = jnp.bfloat16
NT_DIMS = (((1,), (1,)), ((), ()))


def _rms_norm_f32(x, g):
    return x * lax.rsqrt(jnp.mean(x * x, axis=-1, keepdims=True) + EPS) * g


def _const_spec(shape):
    zeros = (0,) * len(shape)
    return pl.BlockSpec(shape, lambda *_: zeros, pipeline_mode=pl.Buffered(1))


def _store_head_rows(v_ref, val, rows):
    for h in range(N_HEADS):
        v_ref[pl.ds(h, rows, stride=N_HEADS), :] = val[:, h * V_DIM:(h + 1) * V_DIM]


def _proj_kernel(x_ref, conv0_ref, h0_ref, gmix_ref, win_ref, wk_ref, wg_ref, bg_ref, convw_ref,
                 convb_ref, lam_ref, wbl_ref,
                 q_ref, k_ref, v_ref, kb_ref, vb_ref, ga_ref, lru_ref, convout_ref, hout_ref,
                 xn_scr, xpad_scr, hc_scr,
                 *, tm, seg_len, tiles_per_seq, keys_position_minor):
    nseg = tm // seg_len
    i = pl.program_id(0)
    hist0 = CONV_PAD - (CONV_WIDTH - 1)
    tail0 = CONV_PAD + seg_len - (CONV_WIDTH - 1)

    if tiles_per_seq > 1:
        @pl.when(i == 0)
        def _():
            xpad_scr[:, 0:CONV_PAD, :] = jnp.zeros((nseg, CONV_PAD, LRU_WIDTH), F32)
            hc_scr[...] = jnp.zeros(hc_scr.shape, F32)

    x = x_ref[...]
    xn_scr[...] = _rms_norm_f32(x, gmix_ref[...]).astype(BF16)

    def proj(c):
        return jnp.dot(xn_scr[...], win_ref[:, c * D_MODEL:(c + 1) * D_MODEL],
                       preferred_element_type=F32)

    q_ref[...] = (proj(0) * Q_SCALE_LOG2).astype(BF16)
    if keys_position_minor:
        kt = lax.dot_general(wk_ref[...], xn_scr[...], NT_DIMS, preferred_element_type=F32)
        k_ref[0] = kt
        kb_ref[0] = kt.astype(BF16)
    else:
        kk = jnp.dot(xn_scr[...], wk_ref[...], preferred_element_type=F32)
        k_ref[...] = kk
        kb_ref[...] = kk.astype(BF16)
    vv = proj(1)
    _store_head_rows(v_ref, vv, tm)
    vb_ref[...] = vv.astype(BF16)
    xl = proj(2)
    for j in range(nseg):
        xpad_scr[j, CONV_PAD:CONV_PAD + seg_len, :] = xl[j * seg_len:(j + 1) * seg_len]
    gelu_g = jax.nn.gelu(proj(3))
    ga_ref[...] = jax.nn.sigmoid(proj(4))
    sig_b = jax.nn.sigmoid(proj(5))

    def stream(j):
        return i // tiles_per_seq if tiles_per_seq > 1 else i * nseg + j

    stream_start = (i % tiles_per_seq) == 0
    for j in range(nseg):
        for t in range(CONV_WIDTH - 1):
            rows = slice(hist0 + t, hist0 + t + 1)
            initial = conv0_ref[t, pl.ds(stream(j), 1), :]
            if tiles_per_seq > 1:
                initial = jnp.where(stream_start, initial, xpad_scr[j, rows, :])
            xpad_scr[j, rows, :] = initial

    convw = convw_ref[...]
    xc_parts = []
    for j in range(nseg):
        acc = convb_ref[...] + convw[0:1, :] * xpad_scr[j, pl.ds(hist0, seg_len), :]
        for t in range(1, CONV_WIDTH):
            acc = acc + convw[t:t + 1, :] * xpad_scr[j, pl.ds(hist0 + t, seg_len), :]
        xc_parts.append(acc)
    xc = xc_parts[0] if nseg == 1 else jnp.concatenate(xc_parts, axis=0)

    xcb = xc.astype(BF16)

    def gate(which):
        cols = [jnp.dot(xcb[:, g * MXU_DEPTH:(g + 1) * MXU_DEPTH], wg_ref[which, g],
                        preferred_element_type=F32) for g in range(N_GATE_GROUPS)]
        return jax.nn.sigmoid(jnp.concatenate(cols, axis=1) + bg_ref[which:which + 1, :])

    r = gate(0)
    ig = gate(1)
    z = -lam_ref[...]
    softplus = jnp.maximum(z, 0.0) + jnp.log1p(jnp.exp(-jnp.abs(z)))
    log_a = (-LRU_C * r) * softplus
    a = jnp.exp(log_a)
    mult = jnp.sqrt(-jnp.tanh(log_a) * (a * a + 1.0))
    u = mult * (ig * xc)

    groups = tm // SUBLANES
    a3 = a.reshape(groups, SUBLANES, LRU_WIDTH)
    u3 = u.reshape(groups, SUBLANES, LRU_WIDTH)
    row = lax.broadcasted_iota(jnp.int32, (1, SUBLANES, LRU_WIDTH), 1)
    shift = 1
    while shift < SUBLANES:
        a_prev = pltpu.roll(a3, shift, axis=1)
        u_prev = pltpu.roll(u3, shift, axis=1)
        keep = row >= shift
        u3 = jnp.where(keep, a3 * u_prev + u3, u3)
        a3 = jnp.where(keep, a3 * a_prev, a3)
        shift *= 2
    seg_groups = seg_len // SUBLANES
    h_groups = []
    for j in range(nseg):
        b = stream(j)
        h_prev = h0_ref[pl.ds(b, 1), :]
        if tiles_per_seq > 1:
            h_prev = jnp.where(stream_start, h_prev, hc_scr[0:1, :])
        for g in range(j * seg_groups, (j + 1) * seg_groups):
            hg = u3[g] + a3[g] * h_prev
            h_groups.append(hg)
            h_prev = hg[SUBLANES - 1:SUBLANES, :]
        hout_ref[pl.ds(b, 1), :] = h_prev
        for t in range(CONV_WIDTH - 1):
            convout_ref[t, pl.ds(b, 1), :] = xpad_scr[j, tail0 + t:tail0 + t + 1, :]
        if tiles_per_seq > 1:
            hc_scr[0:1, :] = h_prev
            xpad_scr[j, hist0:CONV_PAD, :] = xpad_scr[j, tail0:tail0 + CONV_WIDTH - 1, :]
    h = jnp.concatenate(h_groups, axis=0)

    y = (h * gelu_g).astype(BF16)
    lru_ref[...] = sig_b * jnp.dot(y, wbl_ref[...], preferred_element_type=F32)


def _proj_call(x2d, conv0, h0, p, *, seq_len, tm, keys_position_minor):
    n = x2d.shape[0]
    nb = n // seq_len
    seg_len = min(seq_len, tm)
    nseg = tm // seg_len
    tiles_per_seq = seq_len // seg_len
    assert n % tm == 0 and tm % seg_len == 0 and seq_len % seg_len == 0 and seg_len % SUBLANES == 0
    assert nseg == 1 or tiles_per_seq == 1

    row_map = lambda i: (i, 0)
    tok_f32 = jax.ShapeDtypeStruct((n, D_MODEL), F32)
    tok_bf16 = jax.ShapeDtypeStruct((n, D_MODEL), BF16)
    w_k = p["w_kt"] if keys_position_minor else p["w_k"]
    if keys_position_minor:
        kt_map = lambda i: (i // tiles_per_seq, 0, i % tiles_per_seq)
        k_specs = [pl.BlockSpec((1, D_MODEL, tm), kt_map), pl.BlockSpec((1, D_MODEL, tm), kt_map)]
        k_shapes = [jax.ShapeDtypeStruct((nb, D_MODEL, seq_len), F32),
                    jax.ShapeDtypeStruct((nb, D_MODEL, seq_len), BF16)]
    else:
        k_specs = [pl.BlockSpec((tm, D_MODEL), row_map), pl.BlockSpec((tm, D_MODEL), row_map)]
        k_shapes = [tok_f32, tok_bf16]
    kernel = functools.partial(_proj_kernel, tm=tm, seg_len=seg_len, tiles_per_seq=tiles_per_seq,
                               keys_position_minor=keys_position_minor)
    return pl.pallas_call(
        kernel,
        grid=(n // tm,),
        in_specs=[
            pl.BlockSpec((tm, D_MODEL), row_map),
            _const_spec(conv0.shape),
            _const_spec(h0.shape),
            _const_spec((1, D_MODEL)),
            _const_spec(p["w_in"].shape),
            _const_spec(w_k.shape),
            _const_spec(p["w_gate"].shape),
            _const_spec((2, LRU_WIDTH)),
            _const_spec((CONV_WIDTH, LRU_WIDTH)),
            _const_spec((1, LRU_WIDTH)),
            _const_spec((1, LRU_WIDTH)),
            _const_spec((LRU_WIDTH, D_MODEL)),
        ],
        out_specs=[
            pl.BlockSpec((tm, D_MODEL), row_map),
            k_specs[0],
            pl.BlockSpec((tm * N_HEADS, V_DIM), row_map),
            k_specs[1],
            pl.BlockSpec((tm, D_MODEL), row_map),
            pl.BlockSpec((tm, D_MODEL), row_map),
            pl.BlockSpec((tm, D_MODEL), row_map),
            pl.BlockSpec(conv0.shape, lambda i: (0, 0, 0)),
            pl.BlockSpec(h0.shape, lambda i: (0, 0)),
        ],
        out_shape=[tok_bf16, k_shapes[0], jax.ShapeDtypeStruct((n * N_HEADS, V_DIM), F32), k_shapes[1],
                   tok_bf16, tok_f32, tok_f32,
                   jax.ShapeDtypeStruct(conv0.shape, F32), jax.ShapeDtypeStruct(h0.shape, F32)],
        scratch_shapes=[
            pltpu.VMEM((tm, D_MODEL), BF16),
            pltpu.VMEM((nseg, CONV_PAD + seg_len, LRU_WIDTH), F32),
            pltpu.VMEM((SUBLANES, LRU_WIDTH), F32),
        ],
        compiler_params=pltpu.CompilerParams(
            dimension_semantics=("arbitrary",), vmem_limit_bytes=VMEM_LIMIT_BYTES),
        name="proj_lru",
    )(x2d, conv0, h0, p["g_mix"], p["w_in"], w_k, p["w_gate"], p["b_gate"], p["conv_w"],
      p["conv_b"], p["lru_lambda"], p["w_bl"])


def _split_maps(q, tq, q2_scr):
    lane = lax.broadcasted_iota(jnp.int32, q.shape, 1)
    zero = jnp.zeros_like(q)
    q2_scr[0:tq, :] = jnp.where(lane < HEAD_DIM, q, zero)
    q2_scr[tq:2 * tq, :] = jnp.where(lane >= HEAD_DIM, q, zero)


def _lambda_full(lq_ref, lk_ref, lam_init):
    e0 = jnp.exp(jnp.sum(lq_ref[0:1, :] * lk_ref[0:1, :], axis=1, keepdims=True))
    e1 = jnp.exp(jnp.sum(lq_ref[1:2, :] * lk_ref[1:2, :], axis=1, keepdims=True))
    return e0 - e1 + lam_init


def _finish_head(acc, l_row, tq, lam, gain, lam_init):
    o1 = acc[0:tq] / l_row[0:tq]
    o2 = acc[tq:2 * tq] / l_row[tq:2 * tq]
    o = o1 - lam * o2
    o = o * lax.rsqrt(jnp.mean(o * o, axis=-1, keepdims=True) + EPS)
    return o * gain * (1.0 - lam_init)


def _online_step(s, v, m_scr, l_scr, acc_scr):
    nblk = s.shape[1] // LANES
    blocks = [s[:, c * LANES:(c + 1) * LANES] for c in range(nblk)]
    m_prev = m_scr[...]
    m_cur = blocks[0]
    for b in blocks[1:]:
        m_cur = jnp.maximum(m_cur, b)
    m_new = jnp.maximum(m_prev, jnp.max(m_cur, axis=1, keepdims=True))
    alpha = jnp.exp2(m_prev - m_new)
    p_blocks = [jnp.exp2(b - m_new) for b in blocks]
    p_sum = p_blocks[0]
    for pb in p_blocks[1:]:
        p_sum = p_sum + pb
    p = jnp.concatenate(p_blocks, axis=1).astype(BF16)
    l_scr[...] = alpha * l_scr[...] + p_sum
    acc_scr[...] = alpha * acc_scr[...] + jnp.dot(p, v, preferred_element_type=F32)
    m_scr[...] = m_new


def _attn_prompt_kernel(lq_ref, lk_ref, gain_ref, q_ref, kt_ref, v_ref, o_ref,
                        q2_scr, m_scr, l_scr, acc_scr, *, tq, hp, lam_init):
    qi = pl.program_id(2)
    for g in range(hp):
        _split_maps(q_ref[:, g * V_DIM:(g + 1) * V_DIM], tq, q2_scr.at[g])
    m_scr[...] = jnp.full(m_scr.shape, MASK_VALUE, F32)
    l_scr[...] = jnp.zeros(l_scr.shape, F32)
    acc_scr[...] = jnp.zeros(acc_scr.shape, F32)

    def tile(j, visible):
        k0 = pl.multiple_of(j * tq, tq)
        for g in range(hp):
            cols = slice(g * V_DIM, (g + 1) * V_DIM)
            s = jnp.dot(q2_scr[g], kt_ref[0, cols, pl.ds(k0, tq)], preferred_element_type=F32)
            if visible is not None:
                s = jnp.where(visible, s, MASK_VALUE)
            _online_step(s, v_ref[pl.ds(k0, tq), cols], m_scr.at[g], l_scr.at[g], acc_scr.at[g])

    def full_tile(j, carry):
        tile(j, None)
        return carry

    lax.fori_loop(0, qi, full_tile, 0)

    q_chunk = (lax.broadcasted_iota(jnp.int32, (2 * tq, tq), 0) % tq) // CHUNK
    k_chunk = lax.broadcasted_iota(jnp.int32, (2 * tq, tq), 1) // CHUNK
    tile(qi, k_chunk <= q_chunk)

    lam = _lambda_full(lq_ref, lk_ref, lam_init)
    for g in range(hp):
        l_row = jnp.sum(l_scr[g], axis=1, keepdims=True)
        o_ref[:, g * V_DIM:(g + 1) * V_DIM] = _finish_head(
            acc_scr[g], l_row, tq, lam, gain_ref[...], lam_init).astype(o_ref.dtype)


def _attn_prompt_call(q, kbt, vb, p, *, batch, seq_len, tq, hp, lam_init):
    assert seq_len % tq == 0 and tq % CHUNK == 0 and tq % LANES == 0 and N_HEADS % hp == 0
    nq = seq_len // tq
    kernel = functools.partial(_attn_prompt_kernel, tq=tq, hp=hp, lam_init=lam_init)
    return pl.pallas_call(
        kernel,
        grid=(batch, N_HEADS // hp, nq),
        in_specs=[
            _const_spec((2, HEAD_DIM)),
            _const_spec((2, HEAD_DIM)),
            _const_spec((1, V_DIM)),
            pl.BlockSpec((tq, hp * V_DIM), lambda b, h, i: (b * nq + i, h)),
            pl.BlockSpec((1, hp * V_DIM, seq_len), lambda b, h, i: (b, h, 0)),
            pl.BlockSpec((seq_len, hp * V_DIM), lambda b, h, i: (b, h)),
        ],
        out_specs=pl.BlockSpec((tq, hp * V_DIM), lambda b, h, i: (b * nq + i, h)),
        out_shape=jax.ShapeDtypeStruct(q.shape, BF16),
        scratch_shapes=[
            pltpu.VMEM((hp, 2 * tq, V_DIM), BF16),
            pltpu.VMEM((hp, 2 * tq, LANES), F32),
            pltpu.VMEM((hp, 2 * tq, LANES), F32),
            pltpu.VMEM((hp, 2 * tq, V_DIM), F32),
        ],
        compiler_params=pltpu.CompilerParams(
            dimension_semantics=("arbitrary", "arbitrary", "arbitrary"),
            vmem_limit_bytes=VMEM_LIMIT_BYTES),
        name="attn_prompt",
    )(p["lambda_q"], p["lambda_k"], p["head_gain"], q, kbt, vb)


def _attn_sample_kernel(lq_ref, lk_ref, gain_ref, q_ref, kn_ref, vn_ref, kct_ref, vc_ref, o_ref,
                        q2_scr, *, tq, past, lam_init):
    lam = _lambda_full(lq_ref, lk_ref, lam_init)
    for h in range(N_HEADS):
        cols = slice(h * V_DIM, (h + 1) * V_DIM)
        _split_maps(q_ref[:, cols], tq, q2_scr)
        q2 = q2_scr[...]
        s_c = jnp.dot(q2, kct_ref[0, h].astype(BF16), preferred_element_type=F32)
        s_n = lax.dot_general(q2, kn_ref[:, cols], NT_DIMS, preferred_element_type=F32)
        m = jnp.maximum(jnp.max(s_c, axis=1, keepdims=True), jnp.max(s_n, axis=1, keepdims=True))
        p_c = jnp.exp2(s_c - m)
        p_n = jnp.exp2(s_n - m)
        l_row = jnp.sum(p_c, axis=1, keepdims=True) + jnp.sum(p_n, axis=1, keepdims=True)
        v_c = vc_ref[0, pl.ds(h, past, stride=N_HEADS), :].astype(BF16)
        acc = (jnp.dot(p_c.astype(BF16), v_c, preferred_element_type=F32)
               + jnp.dot(p_n.astype(BF16), vn_ref[:, cols], preferred_element_type=F32))
        o_ref[:, cols] = _finish_head(acc, l_row, tq, lam, gain_ref[...], lam_init).astype(o_ref.dtype)


def _attn_sample_call(q, kb, vb, cache_kt, cache_v, p, *, batch, seq_len, lam_init):
    past = cache_kt.shape[3]
    kernel = functools.partial(_attn_sample_kernel, tq=seq_len, past=past, lam_init=lam_init)
    return pl.pallas_call(
        kernel,
        grid=(batch,),
        in_specs=[
            _const_spec((2, HEAD_DIM)),
            _const_spec((2, HEAD_DIM)),
            _const_spec((1, V_DIM)),
            pl.BlockSpec((seq_len, D_MODEL), lambda b: (b, 0)),
            pl.BlockSpec((seq_len, D_MODEL), lambda b: (b, 0)),
            pl.BlockSpec((seq_len, D_MODEL), lambda b: (b, 0)),
            pl.BlockSpec((1, N_HEADS, V_DIM, past), lambda b: (b, 0, 0, 0)),
            pl.BlockSpec((1, past * N_HEADS, V_DIM), lambda b: (b, 0, 0)),
        ],
        out_specs=pl.BlockSpec((seq_len, D_MODEL), lambda b: (b, 0)),
        out_shape=jax.ShapeDtypeStruct(q.shape, BF16),
        scratch_shapes=[pltpu.VMEM((2 * seq_len, V_DIM), BF16)],
        compiler_params=pltpu.CompilerParams(
            dimension_semantics=("arbitrary",), vmem_limit_bytes=VMEM_LIMIT_BYTES),
        name="attn_sample",
    )(p["lambda_q"], p["lambda_k"], p["head_gain"], q, kb, vb, cache_kt, cache_v)


def _merge_kernel(x_ref, o_ref, ga_ref, lru_ref, wba_ref, wo_ref, gmlp_ref, wup_ref, wdown_ref,
                  gfin_ref, y_ref):
    attn = jnp.dot(o_ref[...], wba_ref[...], preferred_element_type=F32)
    merged = ga_ref[...] * attn + lru_ref[...]
    h = x_ref[...] + jnp.dot(merged.astype(BF16), wo_ref[...], preferred_element_type=F32)
    hn = _rms_norm_f32(h, gmlp_ref[...]).astype(BF16)
    mlp = None
    for c in range(D_FF // D_MODEL):
        cols = slice(c * D_MODEL, (c + 1) * D_MODEL)
        up = jnp.dot(hn, wup_ref[:, cols], preferred_element_type=F32)
        act = jnp.square(jnp.maximum(up, 0.0)).astype(BF16)
        part = jnp.dot(act, wdown_ref[cols, :], preferred_element_type=F32)
        mlp = part if mlp is None else mlp + part
    y_ref[...] = _rms_norm_f32(h + mlp, gfin_ref[...])


def _merge_call(x2d, o_attn, ga, lru, p, *, tm):
    n = x2d.shape[0]
    assert n % tm == 0
    row_map = lambda i: (i, 0)
    return pl.pallas_call(
        _merge_kernel,
        grid=(n // tm,),
        in_specs=[
            pl.BlockSpec((tm, D_MODEL), row_map),
            pl.BlockSpec((tm, D_MODEL), row_map),
            pl.BlockSpec((tm, D_MODEL), row_map),
            pl.BlockSpec((tm, D_MODEL), row_map),
            _const_spec((D_MODEL, D_MODEL)),
            _const_spec((D_MODEL, D_MODEL)),
            _const_spec((1, D_MODEL)),
            _const_spec((D_MODEL, D_FF)),
            _const_spec((D_FF, D_MODEL)),
            _const_spec((1, D_MODEL)),
        ],
        out_specs=pl.BlockSpec((tm, D_MODEL), row_map),
        out_shape=jax.ShapeDtypeStruct((n, D_MODEL), F32),
        compiler_params=pltpu.CompilerParams(
            dimension_semantics=("arbitrary",), vmem_limit_bytes=VMEM_LIMIT_BYTES),
        name="merge_mlp",
    )(x2d, o_attn, ga, lru, p["w_ba"], p["w_o"], p["g_mlp"], p["w_up"], p["w_down"], p["g_final"])


def _block_diag_groups(w):
    w4 = w.reshape(N_GATE_GROUPS, GATE_GROUP, LRU_BLOCK, LRU_BLOCK)
    eye = jnp.eye(GATE_GROUP, dtype=w.dtype)
    return jnp.einsum("gaij,ab->gaibj", w4, eye).reshape(N_GATE_GROUPS, MXU_DEPTH, MXU_DEPTH)


def _layer_params(l, norm_mix, norm_mlp, norm_final, w_in, lambda_q, lambda_k, head_gain, conv_w,
                  conv_b, w_rgate, b_rgate, w_igate, b_igate, lru_lambda, w_branch_attn,
                  w_branch_lru, w_out, w_mlp_up, w_mlp_down):
    w_in_b = w_in[l].astype(BF16)
    return {
        "g_mix": norm_mix[l][None, :],
        "g_mlp": norm_mlp[l][None, :],
        "g_final": norm_final[None, :],
        "w_in": jnp.concatenate([w_in_b[:, :D_MODEL], w_in_b[:, 2 * D_MODEL:]], axis=1),
        "w_k": w_in_b[:, D_MODEL:2 * D_MODEL],
        "w_kt": w_in_b[:, D_MODEL:2 * D_MODEL].T,
        "lambda_q": lambda_q[l],
        "lambda_k": lambda_k[l],
        "head_gain": head_gain[l][None, :],
        "conv_w": conv_w[l],
        "conv_b": conv_b[l][None, :],
        "w_gate": jnp.stack([_block_diag_groups(w_rgate[l]), _block_diag_groups(w_igate[l])]).astype(BF16),
        "b_gate": jnp.stack([b_rgate[l], b_igate[l]]),
        "lru_lambda": lru_lambda[l][None, :],
        "w_bl": w_branch_lru[l].astype(BF16),
        "w_ba": w_branch_attn[l].astype(BF16),
        "w_o": w_out[l].astype(BF16),
        "w_up": w_mlp_up[l].astype(BF16),
        "w_down": w_mlp_down[l].astype(BF16),
    }


PROJ_TM = 256
MERGE_TM = 256
ATTN_TQ = 512
ATTN_HEADS_PER_STEP = 4


def kernel(x_prompt, x_sample, cache_k, cache_v, state_conv, state_lru, norm_mix, norm_mlp, norm_final, w_in, lambda_q, lambda_k, head_gain, conv_w, conv_b, w_rgate, b_rgate, w_igate, b_igate, lru_lambda, w_branch_attn, w_branch_lru, w_out, w_mlp_up, w_mlp_down):
    bp, sp, _ = x_prompt.shape
    bs, ss, _ = x_sample.shape
    depth = w_in.shape[0]
    assert depth == 1, "the final norm is fused into the last (only) layer's merge kernel"
    past = cache_k.shape[2]
    assert past % CHUNK == 0 and ss <= CHUNK

    l = 0
    lam_init = 0.8 - 0.6 * math.exp(-0.3 * l)
    p = _layer_params(l, norm_mix, norm_mlp, norm_final, w_in, lambda_q, lambda_k, head_gain, conv_w,
                      conv_b, w_rgate, b_rgate, w_igate, b_igate, lru_lambda, w_branch_attn,
                      w_branch_lru, w_out, w_mlp_up, w_mlp_down)

    xp2 = x_prompt.reshape(bp * sp, D_MODEL)
    conv0_p = jnp.zeros((CONV_WIDTH - 1, bp, LRU_WIDTH), F32)
    h0_p = jnp.zeros((bp, LRU_WIDTH), F32)
    q_p, kt_p, v_p, kbt_p, vb_p, ga_p, lru_p, conv_p, hl_p = _proj_call(
        xp2, conv0_p, h0_p, p, seq_len=sp, tm=PROJ_TM, keys_position_minor=True)
    o_p = _attn_prompt_call(q_p, kbt_p, vb_p, p, batch=bp, seq_len=sp, tq=ATTN_TQ,
                            hp=ATTN_HEADS_PER_STEP, lam_init=lam_init)
    y_p = _merge_call(xp2, o_p, ga_p, lru_p, p, tm=MERGE_TM)

    xs2 = x_sample.reshape(bs * ss, D_MODEL)
    conv0_s = jnp.transpose(state_conv[l], (1, 0, 2))
    q_s, k_s, v_s, kb_s, vb_s, ga_s, lru_s, conv_s, hl_s = _proj_call(
        xs2, conv0_s, state_lru[l], p, seq_len=ss, tm=PROJ_TM, keys_position_minor=False)
    ckt = jnp.transpose(cache_k[l], (0, 2, 3, 4, 1)).reshape(bs, N_HEADS, V_DIM, past)
    cv = cache_v[l].reshape(bs, past * N_HEADS, V_DIM)
    o_s = _attn_sample_call(q_s, kb_s, vb_s, ckt, cv, p, batch=bs, seq_len=ss, lam_init=lam_init)
    y_s = _merge_call(xs2, o_s, ga_s, lru_s, p, tm=MERGE_TM)

    k_prompt = jnp.transpose(kt_p.reshape(bp, N_HEADS, 2, HEAD_DIM, sp), (0, 4, 1, 2, 3))
    return (
        y_p.reshape(bp, sp, D_MODEL),
        y_s.reshape(bs, ss, D_MODEL),
        k_prompt[None],
        v_p.reshape(1, bp, sp, N_HEADS, V_DIM),
        jnp.transpose(conv_p, (1, 0, 2))[None],
        hl_p[None],
        k_s.reshape(1, bs, ss, N_HEADS, 2, HEAD_DIM),
        v_s.reshape(1, bs, ss, N_HEADS, V_DIM),
        jnp.transpose(conv_s, (1, 0, 2))[None],
        hl_s[None],
    )
```

```python
import functools
import math

import jax
import jax.numpy as jnp
import numpy as np
from jax import lax
from jax.experimental import pallas as pl
from jax.experimental.pallas import tpu as pltpu

D_MODEL = 1024
N_HEADS = 8
HEAD_DIM = 64
V_DIM = 2 * HEAD_DIM
CHUNK = 64
LRU_WIDTH = D_MODEL
LRU_BLOCKS = 16
LRU_BLOCK = LRU_WIDTH // LRU_BLOCKS
CONV_WIDTH = 4
LRU_C = 8.0
D_FF = 4 * D_MODEL
EPS = 1e-6

SUBLANES = 8
LANES = 128
MXU_DEPTH = 256
GATE_GROUP = MXU_DEPTH // LRU_BLOCK
N_GATE_GROUPS = LRU_BLOCKS // GATE_GROUP
CONV_PAD = SUBLANES
MASK_VALUE = -0.7 * float(np.finfo(np.float32).max)
Q_SCALE_LOG2 = HEAD_DIM ** -0.5 * math.log2(math.e)
VMEM_LIMIT_BYTES = 56 * 1024 * 1024

F32 = jnp.float32
BF16 = jnp.bfloat16
NT_DIMS = (((1,), (1,)), ((), ()))


def _rms_norm_f32(x, g):
    inv = lax.rsqrt(jnp.sum(x * x, axis=-1, keepdims=True) * (1.0 / x.shape[-1]) + EPS)
    return (x * inv) * g


def _const_spec(shape):
    zeros = (0,) * len(shape)
    return pl.BlockSpec(shape, lambda *_: zeros, pipeline_mode=pl.Buffered(1))


def _store_head_rows(v_ref, val, rows):
    for h in range(N_HEADS):
        v_ref[pl.ds(h, rows, stride=N_HEADS), :] = val[:, h * V_DIM:(h + 1) * V_DIM]


def _proj_kernel(x_ref, conv0_ref, h0_ref, gmix_ref, win_ref, wg_ref, bg_ref, convw_ref,
                 convb_ref, lam_ref, wbl_ref,
                 q_ref, k_ref, v_ref, kb_ref, vb_ref, ga_ref, lru_ref, convout_ref, hout_ref,
                 xn_scr, xpad_scr, hc_scr,
                 *, tm, seg_len, tiles_per_seq, feature_major):
    nseg = tm // seg_len
    i = pl.program_id(0)
    hist0 = CONV_PAD - (CONV_WIDTH - 1)
    tail0 = CONV_PAD + seg_len - (CONV_WIDTH - 1)

    if tiles_per_seq > 1:
        @pl.when(i == 0)
        def _():
            xpad_scr[:, 0:CONV_PAD, :] = jnp.zeros((nseg, CONV_PAD, LRU_WIDTH), F32)
            hc_scr[...] = jnp.zeros(hc_scr.shape, F32)

    x = x_ref[...]
    xn_scr[...] = _rms_norm_f32(x, gmix_ref[...]).astype(BF16)

    def proj(c):
        return jnp.dot(xn_scr[...], win_ref[:, c * D_MODEL:(c + 1) * D_MODEL],
                       preferred_element_type=F32)

    xl = proj(3)
    for j in range(nseg):
        xpad_scr[j, CONV_PAD:CONV_PAD + seg_len, :] = xl[j * seg_len:(j + 1) * seg_len]

    def stream(j):
        return i // tiles_per_seq if tiles_per_seq > 1 else i * nseg + j

    stream_start = (i % tiles_per_seq) == 0
    for j in range(nseg):
        for t in range(CONV_WIDTH - 1):
            rows = slice(hist0 + t, hist0 + t + 1)
            initial = conv0_ref[t, pl.ds(stream(j), 1), :]
            if tiles_per_seq > 1:
                initial = jnp.where(stream_start, initial, xpad_scr[j, rows, :])
            xpad_scr[j, rows, :] = initial

    convw = convw_ref[...]
    xc_parts = []
    for j in range(nseg):
        acc = convb_ref[...] + convw[0:1, :] * xpad_scr[j, pl.ds(hist0, seg_len), :]
        for t in range(1, CONV_WIDTH):
            acc = acc + convw[t:t + 1, :] * xpad_scr[j, pl.ds(hist0 + t, seg_len), :]
        xc_parts.append(acc)
    xc = xc_parts[0] if nseg == 1 else jnp.concatenate(xc_parts, axis=0)

    xcb = xc.astype(BF16)

    def gate(which):
        cols = [jnp.dot(xcb[:, g * MXU_DEPTH:(g + 1) * MXU_DEPTH], wg_ref[which, g],
                        preferred_element_type=F32) for g in range(N_GATE_GROUPS)]
        return jax.nn.sigmoid(jnp.concatenate(cols, axis=1) + bg_ref[which:which + 1, :])

    r = gate(0)
    ig = gate(1)

    qq = proj(0) * Q_SCALE_LOG2
    kk = proj(1)
    vv = proj(2)
    kb_ref[...] = kk.astype(BF16)
    _store_head_rows(v_ref, vv, tm)
    if feature_major:
        q_ref[0] = qq.T.astype(BF16)
        k_ref[0] = kk.T
        vb_ref[0] = vv.T.astype(BF16)
    else:
        q_ref[...] = qq.astype(BF16)
        k_ref[...] = kk
        vb_ref[...] = vv.astype(BF16)
    ga_ref[...] = jax.nn.sigmoid(proj(5))
    sig_b = jax.nn.sigmoid(proj(6))
    gelu_g = jax.nn.gelu(proj(4))

    z = -lam_ref[...]
    softplus = jnp.maximum(z, 0.0) + jnp.log1p(jnp.exp(-jnp.abs(z)))
    log_a = (-LRU_C * r) * softplus
    a = jnp.exp(log_a)
    mult = jnp.sqrt(-jnp.tanh(log_a) * (a * a + 1.0))
    u = mult * (ig * xc)

    groups = tm // SUBLANES
    a3 = a.reshape(groups, SUBLANES, LRU_WIDTH)
    u3 = u.reshape(groups, SUBLANES, LRU_WIDTH)
    row = lax.broadcasted_iota(jnp.int32, (1, SUBLANES, LRU_WIDTH), 1)
    shift = 1
    while shift < SUBLANES:
        a_prev = pltpu.roll(a3, shift, axis=1)
        u_prev = pltpu.roll(u3, shift, axis=1)
        keep = row >= shift
        u3 = jnp.where(keep, a3 * u_prev + u3, u3)
        a3 = jnp.where(keep, a3 * a_prev, a3)
        shift *= 2
    seg_groups = seg_len // SUBLANES
    h_groups = []
    for j in range(nseg):
        b = stream(j)
        h_prev = h0_ref[pl.ds(b, 1), :]
        if tiles_per_seq > 1:
            h_prev = jnp.where(stream_start, h_prev, hc_scr[0:1, :])
        for g in range(j * seg_groups, (j + 1) * seg_groups):
            hg = u3[g] + a3[g] * h_prev
            h_groups.append(hg)
            h_prev = hg[SUBLANES - 1:SUBLANES, :]
        hout_ref[pl.ds(b, 1), :] = h_prev
        for t in range(CONV_WIDTH - 1):
            convout_ref[t, pl.ds(b, 1), :] = xpad_scr[j, tail0 + t:tail0 + t + 1, :]
        if tiles_per_seq > 1:
            hc_scr[0:1, :] = h_prev
            xpad_scr[j, hist0:CONV_PAD, :] = xpad_scr[j, tail0:tail0 + CONV_WIDTH - 1, :]
    h = jnp.concatenate(h_groups, axis=0)

    y = (h * gelu_g).astype(BF16)
    lru_ref[...] = sig_b * jnp.dot(y, wbl_ref[...], preferred_element_type=F32)


def _proj_call(x2d, conv0, h0, p, *, seq_len, tm, feature_major):
    n = x2d.shape[0]
    nb = n // seq_len
    seg_len = min(seq_len, tm)
    nseg = tm // seg_len
    tiles_per_seq = seq_len // seg_len
    assert n % tm == 0 and tm % seg_len == 0 and seq_len % seg_len == 0 and seg_len % SUBLANES == 0
    assert nseg == 1 or tiles_per_seq == 1

    row_map = lambda i: (i, 0)
    tok_f32 = jax.ShapeDtypeStruct((n, D_MODEL), F32)
    tok_bf16 = jax.ShapeDtypeStruct((n, D_MODEL), BF16)
    tok_spec = pl.BlockSpec((tm, D_MODEL), row_map)
    if feature_major:
        fm_spec = pl.BlockSpec((1, D_MODEL, tm), lambda i: (i // tiles_per_seq, 0, i % tiles_per_seq))
        fm_f32 = jax.ShapeDtypeStruct((nb, D_MODEL, seq_len), F32)
        fm_bf16 = jax.ShapeDtypeStruct((nb, D_MODEL, seq_len), BF16)
    else:
        fm_spec, fm_f32, fm_bf16 = tok_spec, tok_f32, tok_bf16
    kernel = functools.partial(_proj_kernel, tm=tm, seg_len=seg_len, tiles_per_seq=tiles_per_seq,
                               feature_major=feature_major)
    return pl.pallas_call(
        kernel,
        grid=(n // tm,),
        in_specs=[
            pl.BlockSpec((tm, D_MODEL), row_map),
            _const_spec(conv0.shape),
            _const_spec(h0.shape),
            _const_spec((1, D_MODEL)),
            _const_spec(p["w_in"].shape),
            _const_spec(p["w_gate"].shape),
            _const_spec((2, LRU_WIDTH)),
            _const_spec((CONV_WIDTH, LRU_WIDTH)),
            _const_spec((1, LRU_WIDTH)),
            _const_spec((1, LRU_WIDTH)),
            _const_spec((LRU_WIDTH, D_MODEL)),
        ],
        out_specs=[
            fm_spec,
            fm_spec,
            pl.BlockSpec((tm * N_HEADS, V_DIM), row_map),
            tok_spec,
            fm_spec,
            tok_spec,
            tok_spec,
            pl.BlockSpec(conv0.shape, lambda i: (0, 0, 0)),
            pl.BlockSpec(h0.shape, lambda i: (0, 0)),
        ],
        out_shape=[fm_bf16, fm_f32, jax.ShapeDtypeStruct((n * N_HEADS, V_DIM), F32), tok_bf16,
                   fm_bf16, tok_f32, tok_f32,
                   jax.ShapeDtypeStruct(conv0.shape, F32), jax.ShapeDtypeStruct(h0.shape, F32)],
        scratch_shapes=[
            pltpu.VMEM((tm, D_MODEL), BF16),
            pltpu.VMEM((nseg, CONV_PAD + seg_len, LRU_WIDTH), F32),
            pltpu.VMEM((SUBLANES, LRU_WIDTH), F32),
        ],
        compiler_params=pltpu.CompilerParams(
            dimension_semantics=("arbitrary",), vmem_limit_bytes=VMEM_LIMIT_BYTES),
        name="proj_lru",
    )(x2d, conv0, h0, p["g_mix"], p["w_in"], p["w_gate"], p["b_gate"], p["conv_w"],
      p["conv_b"], p["lru_lambda"], p["w_bl"])


def _split_maps(q, tq, q2_scr):
    lane = lax.broadcasted_iota(jnp.int32, q.shape, 1)
    zero = jnp.zeros_like(q)
    q2_scr[0:tq, :] = jnp.where(lane < HEAD_DIM, q, zero)
    q2_scr[tq:2 * tq, :] = jnp.where(lane >= HEAD_DIM, q, zero)


def _lambda_full(lq_ref, lk_ref, lam_init):
    e0 = jnp.exp(jnp.sum(lq_ref[0:1, :] * lk_ref[0:1, :], axis=1, keepdims=True))
    e1 = jnp.exp(jnp.sum(lq_ref[1:2, :] * lk_ref[1:2, :], axis=1, keepdims=True))
    return e0 - e1 + lam_init


def _finish_head(acc, l_row, tq, lam, gain, lam_init):
    o1 = acc[0:tq] / l_row[0:tq]
    o2 = acc[tq:2 * tq] / l_row[tq:2 * tq]
    o = o1 - lam * o2
    o = o * lax.rsqrt(jnp.mean(o * o, axis=-1, keepdims=True) + EPS)
    return o * gain * (1.0 - lam_init)


def _online_step_t(s_ref, m8, vt, m_scr, l_scr, acc_scr):
    keys, cols = s_ref.shape
    m_prev = m_scr[...]
    m_new = jnp.maximum(m_prev, jnp.max(m8, axis=0, keepdims=True))
    alpha = jnp.exp2(m_prev - m_new)
    p3 = jnp.exp2(s_ref[...].reshape(keys // SUBLANES, SUBLANES, cols) - m_new)
    l_scr[...] = alpha * l_scr[...] + jnp.sum(p3, axis=0)
    p = p3.reshape(keys, cols).astype(BF16)
    acc_scr[...] = alpha * acc_scr[...] + jnp.dot(vt, p, preferred_element_type=F32)
    m_scr[...] = m_new


def _attn_prompt_kernel(lq_ref, lk_ref, gain_ref, qt_ref, k_ref, vt_ref, o_ref,
                        q2t_scr, s_scr, m_scr, l_scr, acc_scr, *, tq, hp, lam_init):
    qi = pl.program_id(2)
    feat = lax.broadcasted_iota(jnp.int32, (V_DIM, tq), 0)
    for g in range(hp):
        qt = qt_ref[0, g * V_DIM:(g + 1) * V_DIM, :]
        zero = jnp.zeros_like(qt)
        q2t_scr[g, 0] = jnp.where(feat < HEAD_DIM, qt, zero)
        q2t_scr[g, 1] = jnp.where(feat >= HEAD_DIM, qt, zero)
    m_scr[...] = jnp.full(m_scr.shape, MASK_VALUE, F32)
    l_scr[...] = jnp.zeros(l_scr.shape, F32)
    acc_scr[...] = jnp.zeros(acc_scr.shape, F32)
    chains = [(g, c) for g in range(hp) for c in range(2)]

    def run(units):
        def scores(unit, slot):
            g, c, k0, kl, c0, cl, visible = unit
            st = jnp.dot(k_ref[pl.ds(k0, kl), g * V_DIM:(g + 1) * V_DIM], q2t_scr[g, c, :, c0:c0 + cl],
                         preferred_element_type=F32)
            if visible is not None:
                st = jnp.where(visible, st, MASK_VALUE)
            s_scr[slot, 0:kl, 0:cl] = st
            return jnp.max(st.reshape(kl // SUBLANES, SUBLANES, cl), axis=0)

        nslot = SCORE_LOOKAHEAD + 1
        m8s = [scores(u, n % nslot) for n, u in enumerate(units[:SCORE_LOOKAHEAD])]
        for n, (g, c, k0, kl, c0, cl, _) in enumerate(units):
            vt = vt_ref[0, g * V_DIM:(g + 1) * V_DIM, pl.ds(k0, kl)]
            cols = slice(c0, c0 + cl)
            _online_step_t(s_scr.at[n % nslot, 0:kl, 0:cl], m8s[n], vt, m_scr.at[g, c, :, cols],
                           l_scr.at[g, c, :, cols], acc_scr.at[g, c, :, cols])
            if n + SCORE_LOOKAHEAD < len(units):
                m8s.append(scores(units[n + SCORE_LOOKAHEAD], (n + SCORE_LOOKAHEAD) % nslot))

    def full_tile(j, carry):
        k0 = pl.multiple_of(j * tq, tq)
        run([(g, c, k0, tq, 0, tq, None) for g, c in chains])
        return carry

    lax.fori_loop(0, qi, full_tile, 0)

    half = tq // 2
    d0 = pl.multiple_of(qi * tq, tq)
    d1 = pl.multiple_of(qi * tq + half, half)
    k_chunk = lax.broadcasted_iota(jnp.int32, (half, tq), 0) // CHUNK
    q_chunk = lax.broadcasted_iota(jnp.int32, (half, tq), 1) // CHUNK
    first_keys = k_chunk <= q_chunk
    second_keys = first_keys[:, 0:half]
    run([(g, c, d0, half, 0, tq, first_keys) for g, c in chains]
        + [(g, c, d1, half, half, half, second_keys) for g, c in chains])

    lam = _lambda_full(lq_ref, lk_ref, lam_init)
    for g in range(hp):
        o1 = acc_scr[g, 0] / jnp.sum(l_scr[g, 0], axis=0, keepdims=True)
        o2 = acc_scr[g, 1] / jnp.sum(l_scr[g, 1], axis=0, keepdims=True)
        ot = o1 - lam * o2
        ot = ot * lax.rsqrt(jnp.mean(ot * ot, axis=0, keepdims=True) + EPS)
        ot = ot * gain_ref[...] * (1.0 - lam_init)
        o_ref[:, g * V_DIM:(g + 1) * V_DIM] = ot.T.astype(o_ref.dtype)


def _attn_prompt_call(qt, kb, vbt, p, *, batch, seq_len, tq, hp, lam_init):
    assert seq_len % tq == 0 and tq % CHUNK == 0 and tq % LANES == 0 and N_HEADS % hp == 0
    nq = seq_len // tq
    kernel = functools.partial(_attn_prompt_kernel, tq=tq, hp=hp, lam_init=lam_init)
    return pl.pallas_call(
        kernel,
        grid=(batch, N_HEADS // hp, nq),
        in_specs=[
            _const_spec((2, HEAD_DIM)),
            _const_spec((2, HEAD_DIM)),
            _const_spec((V_DIM, 1)),
            pl.BlockSpec((1, hp * V_DIM, tq), lambda b, h, i: (b, h, i)),
            pl.BlockSpec((seq_len, hp * V_DIM), lambda b, h, i: (b, h), pipeline_mode=pl.Buffered(1)),
            pl.BlockSpec((1, hp * V_DIM, seq_len), lambda b, h, i: (b, h, 0), pipeline_mode=pl.Buffered(1)),
        ],
        out_specs=pl.BlockSpec((tq, hp * V_DIM), lambda b, h, i: (b * nq + i, h)),
        out_shape=jax.ShapeDtypeStruct(kb.shape, BF16),
        scratch_shapes=[
            pltpu.VMEM((hp, 2, V_DIM, tq), BF16),
            pltpu.VMEM((SCORE_LOOKAHEAD + 1, tq, tq), F32),
            pltpu.VMEM((hp, 2, 1, tq), F32),
            pltpu.VMEM((hp, 2, SUBLANES, tq), F32),
            pltpu.VMEM((hp, 2, V_DIM, tq), F32),
        ],
        compiler_params=pltpu.CompilerParams(
            dimension_semantics=("arbitrary", "arbitrary", "arbitrary"),
            vmem_limit_bytes=VMEM_LIMIT_BYTES),
        name="attn_prompt",
    )(p["lambda_q"], p["lambda_k"], p["head_gain_col"], qt, kb, vbt)


def _attn_sample_kernel(lq_ref, lk_ref, gain_ref, q_ref, kn_ref, vn_ref, kct_ref, vc_ref, o_ref,
                        q2_scr, *, tq, past, lam_init):
    lam = _lambda_full(lq_ref, lk_ref, lam_init)
    for h in range(N_HEADS):
        cols = slice(h * V_DIM, (h + 1) * V_DIM)
        _split_maps(q_ref[:, cols], tq, q2_scr)
        q2 = q2_scr[...]
        s_c = jnp.dot(q2, kct_ref[0, h].astype(BF16), preferred_element_type=F32)
        s_n = lax.dot_general(q2, kn_ref[:, cols], NT_DIMS, preferred_element_type=F32)
        m = jnp.maximum(jnp.max(s_c, axis=1, keepdims=True), jnp.max(s_n, axis=1, keepdims=True))
        p_c = jnp.exp2(s_c - m)
        p_n = jnp.exp2(s_n - m)
        l_row = jnp.sum(p_c, axis=1, keepdims=True) + jnp.sum(p_n, axis=1, keepdims=True)
        v_c = vc_ref[0, pl.ds(h, past, stride=N_HEADS), :].astype(BF16)
        acc = (jnp.dot(p_c.astype(BF16), v_c, preferred_element_type=F32)
               + jnp.dot(p_n.astype(BF16), vn_ref[:, cols], preferred_element_type=F32))
        o_ref[:, cols] = _finish_head(acc, l_row, tq, lam, gain_ref[...], lam_init).astype(o_ref.dtype)


def _attn_sample_call(q, kb, vb, cache_kt, cache_v, p, *, batch, seq_len, lam_init):
    past = cache_kt.shape[3]
    kernel = functools.partial(_attn_sample_kernel, tq=seq_len, past=past, lam_init=lam_init)
    return pl.pallas_call(
        kernel,
        grid=(batch,),
        in_specs=[
            _const_spec((2, HEAD_DIM)),
            _const_spec((2, HEAD_DIM)),
            _const_spec((1, V_DIM)),
            pl.BlockSpec((seq_len, D_MODEL), lambda b: (b, 0)),
            pl.BlockSpec((seq_len, D_MODEL), lambda b: (b, 0)),
            pl.BlockSpec((seq_len, D_MODEL), lambda b: (b, 0)),
            pl.BlockSpec((1, N_HEADS, V_DIM, past), lambda b: (b, 0, 0, 0)),
            pl.BlockSpec((1, past * N_HEADS, V_DIM), lambda b: (b, 0, 0)),
        ],
        out_specs=pl.BlockSpec((seq_len, D_MODEL), lambda b: (b, 0)),
        out_shape=jax.ShapeDtypeStruct(q.shape, BF16),
        scratch_shapes=[pltpu.VMEM((2 * seq_len, V_DIM), BF16)],
        compiler_params=pltpu.CompilerParams(
            dimension_semantics=("arbitrary",), vmem_limit_bytes=VMEM_LIMIT_BYTES),
        name="attn_sample",
    )(p["lambda_q"], p["lambda_k"], p["head_gain"], q, kb, vb, cache_kt, cache_v)


def _merge_kernel(x_ref, o_ref, ga_ref, lru_ref, wba_ref, wo_ref, gmlp_ref, wup_ref, wdown_ref,
                  gfin_ref, y_ref):
    attn = jnp.dot(o_ref[...], wba_ref[...], preferred_element_type=F32)
    merged = ga_ref[...] * attn + lru_ref[...]
    h = x_ref[...] + jnp.dot(merged.astype(BF16), wo_ref[...], preferred_element_type=F32)
    hn = _rms_norm_f32(h, gmlp_ref[...]).astype(BF16)
    mlp = None
    for c in range(D_FF // D_MODEL):
        cols = slice(c * D_MODEL, (c + 1) * D_MODEL)
        up = jnp.dot(hn, wup_ref[:, cols], preferred_element_type=F32)
        act = jnp.square(jnp.maximum(up, 0.0)).astype(BF16)
        part = jnp.dot(act, wdown_ref[cols, :], preferred_element_type=F32)
        mlp = part if mlp is None else mlp + part
    y_ref[...] = _rms_norm_f32(h + mlp, gfin_ref[...])


def _merge_call(x2d, o_attn, ga, lru, p, *, tm):
    n = x2d.shape[0]
    assert n % tm == 0
    row_map = lambda i: (i, 0)
    return pl.pallas_call(
        _merge_kernel,
        grid=(n // tm,),
        in_specs=[
            pl.BlockSpec((tm, D_MODEL), row_map),
            pl.BlockSpec((tm, D_MODEL), row_map),
            pl.BlockSpec((tm, D_MODEL), row_map),
            pl.BlockSpec((tm, D_MODEL), row_map),
            _const_spec((D_MODEL, D_MODEL)),
            _const_spec((D_MODEL, D_MODEL)),
            _const_spec((1, D_MODEL)),
            _const_spec((D_MODEL, D_FF)),
            _const_spec((D_FF, D_MODEL)),
            _const_spec((1, D_MODEL)),
        ],
        out_specs=pl.BlockSpec((tm, D_MODEL), row_map),
        out_shape=jax.ShapeDtypeStruct((n, D_MODEL), F32),
        compiler_params=pltpu.CompilerParams(
            dimension_semantics=("arbitrary",), vmem_limit_bytes=VMEM_LIMIT_BYTES),
        name="merge_mlp",
    )(x2d, o_attn, ga, lru, p["w_ba"], p["w_o"], p["g_mlp"], p["w_up"], p["w_down"], p["g_final"])


def _block_diag_groups(w):
    w4 = w.reshape(N_GATE_GROUPS, GATE_GROUP, LRU_BLOCK, LRU_BLOCK)
    eye = jnp.eye(GATE_GROUP, dtype=w.dtype)
    return jnp.einsum("gaij,ab->gaibj", w4, eye).reshape(N_GATE_GROUPS, MXU_DEPTH, MXU_DEPTH)


def _layer_params(l, norm_mix, norm_mlp, norm_final, w_in, lambda_q, lambda_k, head_gain, conv_w,
                  conv_b, w_rgate, b_rgate, w_igate, b_igate, lru_lambda, w_branch_attn,
                  w_branch_lru, w_out, w_mlp_up, w_mlp_down):
    return {
        "g_mix": norm_mix[l][None, :],
        "g_mlp": norm_mlp[l][None, :],
        "g_final": norm_final[None, :],
        "w_in": w_in[l].astype(BF16),
        "head_gain_col": head_gain[l][:, None],
        "lambda_q": lambda_q[l],
        "lambda_k": lambda_k[l],
        "head_gain": head_gain[l][None, :],
        "conv_w": conv_w[l],
        "conv_b": conv_b[l][None, :],
        "w_gate": jnp.stack([_block_diag_groups(w_rgate[l]), _block_diag_groups(w_igate[l])]).astype(BF16),
        "b_gate": jnp.stack([b_rgate[l], b_igate[l]]),
        "lru_lambda": lru_lambda[l][None, :],
        "w_bl": w_branch_lru[l].astype(BF16),
        "w_ba": w_branch_attn[l].astype(BF16),
        "w_o": w_out[l].astype(BF16),
        "w_up": w_mlp_up[l].astype(BF16),
        "w_down": w_mlp_down[l].astype(BF16),
    }


PROJ_TM = 256
MERGE_TM = 256
ATTN_TQ = 512
ATTN_HEADS_PER_STEP = 8
SCORE_LOOKAHEAD = 4


def kernel(x_prompt, x_sample, cache_k, cache_v, state_conv, state_lru, norm_mix, norm_mlp, norm_final, w_in, lambda_q, lambda_k, head_gain, conv_w, conv_b, w_rgate, b_rgate, w_igate, b_igate, lru_lambda, w_branch_attn, w_branch_lru, w_out, w_mlp_up, w_mlp_down):
    bp, sp, _ = x_prompt.shape
    bs, ss, _ = x_sample.shape
    depth = w_in.shape[0]
    assert depth == 1, "the final norm is fused into the last (only) layer's merge kernel"
    past = cache_k.shape[2]
    assert past % CHUNK == 0 and ss <= CHUNK

    l = 0
    lam_init = 0.8 - 0.6 * math.exp(-0.3 * l)
    p = _layer_params(l, norm_mix, norm_mlp, norm_final, w_in, lambda_q, lambda_k, head_gain, conv_w,
                      conv_b, w_rgate, b_rgate, w_igate, b_igate, lru_lambda, w_branch_attn,
                      w_branch_lru, w_out, w_mlp_up, w_mlp_down)

    xp2 = x_prompt.reshape(bp * sp, D_MODEL)
    conv0_p = jnp.zeros((CONV_WIDTH - 1, bp, LRU_WIDTH), F32)
    h0_p = jnp.zeros((bp, LRU_WIDTH), F32)
    qt_p, kt_p, v_p, kb_p, vbt_p, ga_p, lru_p, conv_p, hl_p = _proj_call(
        xp2, conv0_p, h0_p, p, seq_len=sp, tm=PROJ_TM, feature_major=True)
    o_p = _attn_prompt_call(qt_p, kb_p, vbt_p, p, batch=bp, seq_len=sp, tq=ATTN_TQ,
                            hp=ATTN_HEADS_PER_STEP, lam_init=lam_init)
    y_p = _merge_call(xp2, o_p, ga_p, lru_p, p, tm=MERGE_TM)

    xs2 = x_sample.reshape(bs * ss, D_MODEL)
    conv0_s = jnp.transpose(state_conv[l], (1, 0, 2))
    q_s, k_s, v_s, kb_s, vb_s, ga_s, lru_s, conv_s, hl_s = _proj_call(
        xs2, conv0_s, state_lru[l], p, seq_len=ss, tm=PROJ_TM, feature_major=False)
    ckt = jnp.transpose(cache_k[l], (0, 2, 3, 4, 1)).reshape(bs, N_HEADS, V_DIM, past)
    cv = cache_v[l].reshape(bs, past * N_HEADS, V_DIM)
    o_s = _attn_sample_call(q_s, kb_s, vb_s, ckt, cv, p, batch=bs, seq_len=ss, lam_init=lam_init)
    y_s = _merge_call(xs2, o_s, ga_s, lru_s, p, tm=MERGE_TM)

    k_prompt = jnp.transpose(kt_p.reshape(bp, N_HEADS, 2, HEAD_DIM, sp), (0, 4, 1, 2, 3))
    return (
        y_p.reshape(bp, sp, D_MODEL),
        y_s.reshape(bs, ss, D_MODEL),
        k_prompt[None],
        v_p.reshape(1, bp, sp, N_HEADS, V_DIM),
        jnp.transpose(conv_p, (1, 0, 2))[None],
        hl_p[None],
        k_s.reshape(1, bs, ss, N_HEADS, 2, HEAD_DIM),
        v_s.reshape(1, bs, ss, N_HEADS, V_DIM),
        jnp.transpose(conv_s, (1, 0, 2))[None],
        hl_s[None],
    )
```

```python
import functools
import math

import jax
import jax.numpy as jnp
import numpy as np
from jax import lax
from jax.experimental import pallas as pl
from jax.experimental.pallas import tpu as pltpu

D_MODEL = 1024
N_HEADS = 8
HEAD_DIM = 64
V_DIM = 2 * HEAD_DIM
CHUNK = 64
LRU_WIDTH = D_MODEL
LRU_BLOCKS = 16
LRU_BLOCK = LRU_WIDTH // LRU_BLOCKS
CONV_WIDTH = 4
LRU_C = 8.0
D_FF = 4 * D_MODEL
EPS = 1e-6

SUBLANES = 8
LANES = 128
MXU_DEPTH = 256
GATE_GROUP = MXU_DEPTH // LRU_BLOCK
N_GATE_GROUPS = LRU_BLOCKS // GATE_GROUP
CONV_PAD = SUBLANES
MASK_VALUE = -0.7 * float(np.finfo(np.float32).max)
Q_SCALE_LOG2 = HEAD_DIM ** -0.5 * math.log2(math.e)
VMEM_LIMIT_BYTES = 56 * 1024 * 1024

PROJ_COLUMNS = ("q", "k", "v", "x_lru", "g_lru", "gate_a", "gate_b")
PARKED = ("q", "k", "v", "g_lru", "gate_a", "gate_b")

F32 = jnp.float32
BF16 = jnp.bfloat16
NT_DIMS = (((1,), (1,)), ((), ()))


def _rms_norm_f32(x, g):
    inv = lax.rsqrt(jnp.sum(x * x, axis=-1, keepdims=True) * (1.0 / x.shape[-1]) + EPS)
    return (x * inv) * g


def _const_spec(shape):
    zeros = (0,) * len(shape)
    return pl.BlockSpec(shape, lambda *_: zeros, pipeline_mode=pl.Buffered(1))


def _store_head_rows(v_ref, val, rows):
    for h in range(N_HEADS):
        v_ref[pl.ds(h, rows, stride=N_HEADS), :] = val[:, h * V_DIM:(h + 1) * V_DIM]


def _proj_kernel(x_ref, conv0_ref, h0_ref, gmix_ref, win_ref, wg_ref, bg_ref, convw_ref,
                 convb_ref, lam_ref, wbl_ref,
                 q_ref, k_ref, v_ref, kb_ref, vb_ref, ga_ref, lru_ref, convout_ref, hout_ref,
                 xn_scr, xpad_scr, hc_scr, raw_scr,
                 *, tm, seg_len, tiles_per_seq, feature_major):
    nseg = tm // seg_len
    i = pl.program_id(0)
    hist0 = CONV_PAD - (CONV_WIDTH - 1)
    tail0 = CONV_PAD + seg_len - (CONV_WIDTH - 1)

    if tiles_per_seq > 1:
        @pl.when(i == 0)
        def _():
            xpad_scr[:, 0:CONV_PAD, :] = jnp.zeros((nseg, CONV_PAD, LRU_WIDTH), F32)
            hc_scr[...] = jnp.zeros(hc_scr.shape, F32)

    x = x_ref[...]
    xn_scr[...] = _rms_norm_f32(x, gmix_ref[...]).astype(BF16)

    def proj(c):
        return jnp.dot(xn_scr[...], win_ref[:, c * D_MODEL:(c + 1) * D_MODEL],
                       preferred_element_type=F32)

    def park(name):
        raw_scr[PARKED.index(name)] = proj(PROJ_COLUMNS.index(name))

    def parked(name):
        return raw_scr[PARKED.index(name)]

    xl = proj(PROJ_COLUMNS.index("x_lru"))
    for j in range(nseg):
        xpad_scr[j, CONV_PAD:CONV_PAD + seg_len, :] = xl[j * seg_len:(j + 1) * seg_len]
    park("q")
    park("k")

    def stream(j):
        return i // tiles_per_seq if tiles_per_seq > 1 else i * nseg + j

    stream_start = (i % tiles_per_seq) == 0
    for j in range(nseg):
        for t in range(CONV_WIDTH - 1):
            rows = slice(hist0 + t, hist0 + t + 1)
            initial = conv0_ref[t, pl.ds(stream(j), 1), :]
            if tiles_per_seq > 1:
                initial = jnp.where(stream_start, initial, xpad_scr[j, rows, :])
            xpad_scr[j, rows, :] = initial

    convw = convw_ref[...]
    xc_parts = []
    for j in range(nseg):
        acc = convb_ref[...] + convw[0:1, :] * xpad_scr[j, pl.ds(hist0, seg_len), :]
        for t in range(1, CONV_WIDTH):
            acc = acc + convw[t:t + 1, :] * xpad_scr[j, pl.ds(hist0 + t, seg_len), :]
        xc_parts.append(acc)
    xc = xc_parts[0] if nseg == 1 else jnp.concatenate(xc_parts, axis=0)

    xcb = xc.astype(BF16)

    def gate(which):
        cols = [jnp.dot(xcb[:, g * MXU_DEPTH:(g + 1) * MXU_DEPTH], wg_ref[which, g],
                        preferred_element_type=F32) for g in range(N_GATE_GROUPS)]
        return jax.nn.sigmoid(jnp.concatenate(cols, axis=1) + bg_ref[which:which + 1, :])

    r = gate(0)
    ig = gate(1)

    for name in ("v", "g_lru", "gate_a", "gate_b"):
        park(name)
    qq = parked("q") * Q_SCALE_LOG2
    kk = parked("k")
    vv = parked("v")
    kb_ref[...] = kk.astype(BF16)
    _store_head_rows(v_ref, vv, tm)
    if feature_major:
        q_ref[0] = qq.T.astype(BF16)
        k_ref[0] = kk.T
        vb_ref[0] = vv.T.astype(BF16)
    else:
        q_ref[...] = qq.astype(BF16)
        k_ref[...] = kk
        vb_ref[...] = vv.astype(BF16)
    ga_ref[...] = jax.nn.sigmoid(parked("gate_a"))
    sig_b = jax.nn.sigmoid(parked("gate_b"))
    gelu_g = jax.nn.gelu(parked("g_lru"))

    z = -lam_ref[...]
    softplus = jnp.maximum(z, 0.0) + jnp.log1p(jnp.exp(-jnp.abs(z)))
    log_a = (-LRU_C * r) * softplus
    a = jnp.exp(log_a)
    mult = jnp.sqrt(-jnp.tanh(log_a) * (a * a + 1.0))
    u = mult * (ig * xc)

    groups = tm // SUBLANES
    a3 = a.reshape(groups, SUBLANES, LRU_WIDTH)
    u3 = u.reshape(groups, SUBLANES, LRU_WIDTH)
    row = lax.broadcasted_iota(jnp.int32, (1, SUBLANES, LRU_WIDTH), 1)
    shift = 1
    while shift < SUBLANES:
        a_prev = pltpu.roll(a3, shift, axis=1)
        u_prev = pltpu.roll(u3, shift, axis=1)
        keep = row >= shift
        u3 = jnp.where(keep, a3 * u_prev + u3, u3)
        a3 = jnp.where(keep, a3 * a_prev, a3)
        shift *= 2
    seg_groups = seg_len // SUBLANES
    h_groups = []
    for j in range(nseg):
        b = stream(j)
        h_prev = h0_ref[pl.ds(b, 1), :]
        if tiles_per_seq > 1:
            h_prev = jnp.where(stream_start, h_prev, hc_scr[0:1, :])
        for g in range(j * seg_groups, (j + 1) * seg_groups):
            hg = u3[g] + a3[g] * h_prev
            h_groups.append(hg)
            h_prev = hg[SUBLANES - 1:SUBLANES, :]
        hout_ref[pl.ds(b, 1), :] = h_prev
        for t in range(CONV_WIDTH - 1):
            convout_ref[t, pl.ds(b, 1), :] = xpad_scr[j, tail0 + t:tail0 + t + 1, :]
        if tiles_per_seq > 1:
            hc_scr[0:1, :] = h_prev
            xpad_scr[j, hist0:CONV_PAD, :] = xpad_scr[j, tail0:tail0 + CONV_WIDTH - 1, :]
    h = jnp.concatenate(h_groups, axis=0)

    y = (h * gelu_g).astype(BF16)
    lru_ref[...] = sig_b * jnp.dot(y, wbl_ref[...], preferred_element_type=F32)


def _proj_call(x2d, conv0, h0, p, *, seq_len, tm, feature_major):
    n = x2d.shape[0]
    nb = n // seq_len
    seg_len = min(seq_len, tm)
    nseg = tm // seg_len
    tiles_per_seq = seq_len // seg_len
    assert n % tm == 0 and tm % seg_len == 0 and seq_len % seg_len == 0 and seg_len % SUBLANES == 0
    assert nseg == 1 or tiles_per_seq == 1

    row_map = lambda i: (i, 0)
    tok_f32 = jax.ShapeDtypeStruct((n, D_MODEL), F32)
    tok_bf16 = jax.ShapeDtypeStruct((n, D_MODEL), BF16)
    tok_spec = pl.BlockSpec((tm, D_MODEL), row_map)
    if feature_major:
        fm_spec = pl.BlockSpec((1, D_MODEL, tm), lambda i: (i // tiles_per_seq, 0, i % tiles_per_seq))
        fm_f32 = jax.ShapeDtypeStruct((nb, D_MODEL, seq_len), F32)
        fm_bf16 = jax.ShapeDtypeStruct((nb, D_MODEL, seq_len), BF16)
    else:
        fm_spec, fm_f32, fm_bf16 = tok_spec, tok_f32, tok_bf16
    kernel = functools.partial(_proj_kernel, tm=tm, seg_len=seg_len, tiles_per_seq=tiles_per_seq,
                               feature_major=feature_major)
    return pl.pallas_call(
        kernel,
        grid=(n // tm,),
        in_specs=[
            pl.BlockSpec((tm, D_MODEL), row_map),
            _const_spec(conv0.shape),
            _const_spec(h0.shape),
            _const_spec((1, D_MODEL)),
            _const_spec(p["w_in"].shape),
            _const_spec(p["w_gate"].shape),
            _const_spec((2, LRU_WIDTH)),
            _const_spec((CONV_WIDTH, LRU_WIDTH)),
            _const_spec((1, LRU_WIDTH)),
            _const_spec((1, LRU_WIDTH)),
            _const_spec((LRU_WIDTH, D_MODEL)),
        ],
        out_specs=[
            fm_spec,
            fm_spec,
            pl.BlockSpec((tm * N_HEADS, V_DIM), row_map),
            tok_spec,
            fm_spec,
            tok_spec,
            tok_spec,
            pl.BlockSpec(conv0.shape, lambda i: (0, 0, 0)),
            pl.BlockSpec(h0.shape, lambda i: (0, 0)),
        ],
        out_shape=[fm_bf16, fm_f32, jax.ShapeDtypeStruct((n * N_HEADS, V_DIM), F32), tok_bf16,
                   fm_bf16, tok_f32, tok_f32,
                   jax.ShapeDtypeStruct(conv0.shape, F32), jax.ShapeDtypeStruct(h0.shape, F32)],
        scratch_shapes=[
            pltpu.VMEM((tm, D_MODEL), BF16),
            pltpu.VMEM((nseg, CONV_PAD + seg_len, LRU_WIDTH), F32),
            pltpu.VMEM((SUBLANES, LRU_WIDTH), F32),
            pltpu.VMEM((len(PARKED), tm, D_MODEL), F32),
        ],
        compiler_params=pltpu.CompilerParams(
            dimension_semantics=("arbitrary",), vmem_limit_bytes=VMEM_LIMIT_BYTES),
        name="proj_lru",
    )(x2d, conv0, h0, p["g_mix"], p["w_in"], p["w_gate"], p["b_gate"], p["conv_w"],
      p["conv_b"], p["lru_lambda"], p["w_bl"])


def _split_maps(q, tq, q2_scr):
    lane = lax.broadcasted_iota(jnp.int32, q.shape, 1)
    zero = jnp.zeros_like(q)
    q2_scr[0:tq, :] = jnp.where(lane < HEAD_DIM, q, zero)
    q2_scr[tq:2 * tq, :] = jnp.where(lane >= HEAD_DIM, q, zero)


def _lambda_full(lq_ref, lk_ref, lam_init):
    e0 = jnp.exp(jnp.sum(lq_ref[0:1, :] * lk_ref[0:1, :], axis=1, keepdims=True))
    e1 = jnp.exp(jnp.sum(lq_ref[1:2, :] * lk_ref[1:2, :], axis=1, keepdims=True))
    return e0 - e1 + lam_init


def _finish_head(acc, l_row, tq, lam, gain, lam_init):
    o1 = acc[0:tq] / l_row[0:tq]
    o2 = acc[tq:2 * tq] / l_row[tq:2 * tq]
    o = o1 - lam * o2
    o = o * lax.rsqrt(jnp.mean(o * o, axis=-1, keepdims=True) + EPS)
    return o * gain * (1.0 - lam_init)


def _online_step_t(s_ref, m8, vt, m_scr, l_scr, acc_scr):
    keys, cols = s_ref.shape
    m_prev = m_scr[...]
    m_new = jnp.maximum(m_prev, jnp.max(m8, axis=0, keepdims=True))
    alpha = jnp.exp2(m_prev - m_new)
    p3 = jnp.exp2(s_ref[...].reshape(keys // SUBLANES, SUBLANES, cols) - m_new)
    l_scr[...] = alpha * l_scr[...] + jnp.sum(p3, axis=0)
    p = p3.reshape(keys, cols).astype(BF16)
    acc_scr[...] = alpha * acc_scr[...] + jnp.dot(vt, p, preferred_element_type=F32)
    m_scr[...] = m_new


def _attn_prompt_kernel(lq_ref, lk_ref, gain_ref, qt_ref, k_ref, vt_ref, o_ref,
                        q2t_scr, s_scr, m_scr, l_scr, acc_scr, *, tq, hp, lam_init):
    qi = pl.program_id(2)
    feat = lax.broadcasted_iota(jnp.int32, (V_DIM, tq), 0)
    for g in range(hp):
        qt = qt_ref[0, g * V_DIM:(g + 1) * V_DIM, :]
        zero = jnp.zeros_like(qt)
        q2t_scr[g, 0] = jnp.where(feat < HEAD_DIM, qt, zero)
        q2t_scr[g, 1] = jnp.where(feat >= HEAD_DIM, qt, zero)
    m_scr[...] = jnp.full(m_scr.shape, MASK_VALUE, F32)
    l_scr[...] = jnp.zeros(l_scr.shape, F32)
    acc_scr[...] = jnp.zeros(acc_scr.shape, F32)
    chains = [(g, c) for g in range(hp) for c in range(2)]

    def run(units):
        def scores(unit, slot):
            g, c, k0, kl, c0, cl, visible = unit
            st = jnp.dot(k_ref[pl.ds(k0, kl), g * V_DIM:(g + 1) * V_DIM], q2t_scr[g, c, :, c0:c0 + cl],
                         preferred_element_type=F32)
            if visible is not None:
                st = jnp.where(visible, st, MASK_VALUE)
            s_scr[slot, 0:kl, 0:cl] = st
            return jnp.max(st.reshape(kl // SUBLANES, SUBLANES, cl), axis=0)

        nslot = SCORE_LOOKAHEAD + 1
        m8s = [scores(u, n % nslot) for n, u in enumerate(units[:SCORE_LOOKAHEAD])]
        for n, (g, c, k0, kl, c0, cl, _) in enumerate(units):
            vt = vt_ref[0, g * V_DIM:(g + 1) * V_DIM, pl.ds(k0, kl)]
            cols = slice(c0, c0 + cl)
            _online_step_t(s_scr.at[n % nslot, 0:kl, 0:cl], m8s[n], vt, m_scr.at[g, c, :, cols],
                           l_scr.at[g, c, :, cols], acc_scr.at[g, c, :, cols])
            if n + SCORE_LOOKAHEAD < len(units):
                m8s.append(scores(units[n + SCORE_LOOKAHEAD], (n + SCORE_LOOKAHEAD) % nslot))

    def full_tile(j, carry):
        k0 = pl.multiple_of(j * tq, tq)
        run([(g, c, k0, tq, 0, tq, None) for g, c in chains])
        return carry

    lax.fori_loop(0, qi, full_tile, 0)

    half = tq // 2
    d0 = pl.multiple_of(qi * tq, tq)
    d1 = pl.multiple_of(qi * tq + half, half)
    k_chunk = lax.broadcasted_iota(jnp.int32, (half, tq), 0) // CHUNK
    q_chunk = lax.broadcasted_iota(jnp.int32, (half, tq), 1) // CHUNK
    first_keys = k_chunk <= q_chunk
    second_keys = first_keys[:, 0:half]
    run([(g, c, d0, half, 0, tq, first_keys) for g, c in chains]
        + [(g, c, d1, half, half, half, second_keys) for g, c in chains])

    lam = _lambda_full(lq_ref, lk_ref, lam_init)
    for g in range(hp):
        o1 = acc_scr[g, 0] / jnp.sum(l_scr[g, 0], axis=0, keepdims=True)
        o2 = acc_scr[g, 1] / jnp.sum(l_scr[g, 1], axis=0, keepdims=True)
        ot = o1 - lam * o2
        ot = ot * lax.rsqrt(jnp.mean(ot * ot, axis=0, keepdims=True) + EPS)
        ot = ot * gain_ref[...] * (1.0 - lam_init)
        o_ref[:, g * V_DIM:(g + 1) * V_DIM] = ot.T.astype(o_ref.dtype)


def _attn_prompt_call(qt, kb, vbt, p, *, batch, seq_len, tq, hp, lam_init):
    assert seq_len % tq == 0 and tq % CHUNK == 0 and tq % LANES == 0 and N_HEADS % hp == 0
    nq = seq_len // tq
    kernel = functools.partial(_attn_prompt_kernel, tq=tq, hp=hp, lam_init=lam_init)
    return pl.pallas_call(
        kernel,
        grid=(batch, N_HEADS // hp, nq),
        in_specs=[
            _const_spec((2, HEAD_DIM)),
            _const_spec((2, HEAD_DIM)),
            _const_spec((V_DIM, 1)),
            pl.BlockSpec((1, hp * V_DIM, tq), lambda b, h, i: (b, h, i)),
            pl.BlockSpec((seq_len, hp * V_DIM), lambda b, h, i: (b, h), pipeline_mode=pl.Buffered(1)),
            pl.BlockSpec((1, hp * V_DIM, seq_len), lambda b, h, i: (b, h, 0), pipeline_mode=pl.Buffered(1)),
        ],
        out_specs=pl.BlockSpec((tq, hp * V_DIM), lambda b, h, i: (b * nq + i, h)),
        out_shape=jax.ShapeDtypeStruct(kb.shape, BF16),
        scratch_shapes=[
            pltpu.VMEM((hp, 2, V_DIM, tq), BF16),
            pltpu.VMEM((SCORE_LOOKAHEAD + 1, tq, tq), F32),
            pltpu.VMEM((hp, 2, 1, tq), F32),
            pltpu.VMEM((hp, 2, SUBLANES, tq), F32),
            pltpu.VMEM((hp, 2, V_DIM, tq), F32),
        ],
        compiler_params=pltpu.CompilerParams(
            dimension_semantics=("arbitrary", "arbitrary", "arbitrary"),
            vmem_limit_bytes=VMEM_LIMIT_BYTES),
        name="attn_prompt",
    )(p["lambda_q"], p["lambda_k"], p["head_gain_col"], qt, kb, vbt)


def _attn_sample_kernel(lq_ref, lk_ref, gain_ref, q_ref, kn_ref, vn_ref, kct_ref, vc_ref, o_ref,
                        q2_scr, *, tq, past, lam_init):
    lam = _lambda_full(lq_ref, lk_ref, lam_init)

    def scores(h):
        cols = slice(h * V_DIM, (h + 1) * V_DIM)
        _split_maps(q_ref[:, cols], tq, q2_scr.at[h])
        q2 = q2_scr[h]
        s_c = jnp.dot(q2, kct_ref[0, h].astype(BF16), preferred_element_type=F32)
        s_n = lax.dot_general(q2, kn_ref[:, cols], NT_DIMS, preferred_element_type=F32)
        return s_c, s_n

    ahead = [scores(h) for h in range(SAMPLE_LOOKAHEAD)]
    for h in range(N_HEADS):
        cols = slice(h * V_DIM, (h + 1) * V_DIM)
        s_c, s_n = ahead[h]
        if h + SAMPLE_LOOKAHEAD < N_HEADS:
            ahead.append(scores(h + SAMPLE_LOOKAHEAD))
        m = jnp.maximum(jnp.max(s_c, axis=1, keepdims=True), jnp.max(s_n, axis=1, keepdims=True))
        p_c = jnp.exp2(s_c - m)
        p_n = jnp.exp2(s_n - m)
        l_row = jnp.sum(p_c, axis=1, keepdims=True) + jnp.sum(p_n, axis=1, keepdims=True)
        v_c = vc_ref[0, pl.ds(h, past, stride=N_HEADS), :].astype(BF16)
        acc = (jnp.dot(p_c.astype(BF16), v_c, preferred_element_type=F32)
               + jnp.dot(p_n.astype(BF16), vn_ref[:, cols], preferred_element_type=F32))
        o_ref[:, cols] = _finish_head(acc, l_row, tq, lam, gain_ref[...], lam_init).astype(o_ref.dtype)


def _attn_sample_call(q, kb, vb, cache_kt, cache_v, p, *, batch, seq_len, lam_init):
    past = cache_kt.shape[3]
    kernel = functools.partial(_attn_sample_kernel, tq=seq_len, past=past, lam_init=lam_init)
    return pl.pallas_call(
        kernel,
        grid=(batch,),
        in_specs=[
            _const_spec((2, HEAD_DIM)),
            _const_spec((2, HEAD_DIM)),
            _const_spec((1, V_DIM)),
            pl.BlockSpec((seq_len, D_MODEL), lambda b: (b, 0)),
            pl.BlockSpec((seq_len, D_MODEL), lambda b: (b, 0)),
            pl.BlockSpec((seq_len, D_MODEL), lambda b: (b, 0)),
            pl.BlockSpec((1, N_HEADS, V_DIM, past), lambda b: (b, 0, 0, 0)),
            pl.BlockSpec((1, past * N_HEADS, V_DIM), lambda b: (b, 0, 0)),
        ],
        out_specs=pl.BlockSpec((seq_len, D_MODEL), lambda b: (b, 0)),
        out_shape=jax.ShapeDtypeStruct(q.shape, BF16),
        scratch_shapes=[pltpu.VMEM((N_HEADS, 2 * seq_len, V_DIM), BF16)],
        compiler_params=pltpu.CompilerParams(
            dimension_semantics=("arbitrary",), vmem_limit_bytes=VMEM_LIMIT_BYTES),
        name="attn_sample",
    )(p["lambda_q"], p["lambda_k"], p["head_gain"], q, kb, vb, cache_kt, cache_v)


def _merge_kernel(x_ref, o_ref, ga_ref, lru_ref, wba_ref, wo_ref, gmlp_ref, wup_ref, wdown_ref,
                  gfin_ref, y_ref):
    attn = jnp.dot(o_ref[...], wba_ref[...], preferred_element_type=F32)
    merged = ga_ref[...] * attn + lru_ref[...]
    h = x_ref[...] + jnp.dot(merged.astype(BF16), wo_ref[...], preferred_element_type=F32)
    hn = _rms_norm_f32(h, gmlp_ref[...]).astype(BF16)
    mlp = None
    for c in range(D_FF // D_MODEL):
        cols = slice(c * D_MODEL, (c + 1) * D_MODEL)
        up = jnp.dot(hn, wup_ref[:, cols], preferred_element_type=F32)
        act = jnp.square(jnp.maximum(up, 0.0)).astype(BF16)
        part = jnp.dot(act, wdown_ref[cols, :], preferred_element_type=F32)
        mlp = part if mlp is None else mlp + part
    y_ref[...] = _rms_norm_f32(h + mlp, gfin_ref[...])


def _merge_call(x2d, o_attn, ga, lru, p, *, tm):
    n = x2d.shape[0]
    assert n % tm == 0
    row_map = lambda i: (i, 0)
    return pl.pallas_call(
        _merge_kernel,
        grid=(n // tm,),
        in_specs=[
            pl.BlockSpec((tm, D_MODEL), row_map),
            pl.BlockSpec((tm, D_MODEL), row_map),
            pl.BlockSpec((tm, D_MODEL), row_map),
            pl.BlockSpec((tm, D_MODEL), row_map),
            _const_spec((D_MODEL, D_MODEL)),
            _const_spec((D_MODEL, D_MODEL)),
            _const_spec((1, D_MODEL)),
            _const_spec((D_MODEL, D_FF)),
            _const_spec((D_FF, D_MODEL)),
            _const_spec((1, D_MODEL)),
        ],
        out_specs=pl.BlockSpec((tm, D_MODEL), row_map),
        out_shape=jax.ShapeDtypeStruct((n, D_MODEL), F32),
        compiler_params=pltpu.CompilerParams(
            dimension_semantics=("arbitrary",), vmem_limit_bytes=VMEM_LIMIT_BYTES),
        name="merge_mlp",
    )(x2d, o_attn, ga, lru, p["w_ba"], p["w_o"], p["g_mlp"], p["w_up"], p["w_down"], p["g_final"])


def _block_diag_groups(w):
    w4 = w.reshape(N_GATE_GROUPS, GATE_GROUP, LRU_BLOCK, LRU_BLOCK)
    eye = jnp.eye(GATE_GROUP, dtype=w.dtype)
    return jnp.einsum("gaij,ab->gaibj", w4, eye).reshape(N_GATE_GROUPS, MXU_DEPTH, MXU_DEPTH)


def _layer_params(l, norm_mix, norm_mlp, norm_final, w_in, lambda_q, lambda_k, head_gain, conv_w,
                  conv_b, w_rgate, b_rgate, w_igate, b_igate, lru_lambda, w_branch_attn,
                  w_branch_lru, w_out, w_mlp_up, w_mlp_down):
    return {
        "g_mix": norm_mix[l][None, :],
        "g_mlp": norm_mlp[l][None, :],
        "g_final": norm_final[None, :],
        "w_in": w_in[l].astype(BF16),
        "head_gain_col": head_gain[l][:, None],
        "lambda_q": lambda_q[l],
        "lambda_k": lambda_k[l],
        "head_gain": head_gain[l][None, :],
        "conv_w": conv_w[l],
        "conv_b": conv_b[l][None, :],
        "w_gate": jnp.stack([_block_diag_groups(w_rgate[l]), _block_diag_groups(w_igate[l])]).astype(BF16),
        "b_gate": jnp.stack([b_rgate[l], b_igate[l]]),
        "lru_lambda": lru_lambda[l][None, :],
        "w_bl": w_branch_lru[l].astype(BF16),
        "w_ba": w_branch_attn[l].astype(BF16),
        "w_o": w_out[l].astype(BF16),
        "w_up": w_mlp_up[l].astype(BF16),
        "w_down": w_mlp_down[l].astype(BF16),
    }


PROJ_TM = 256
MERGE_TM = 256
ATTN_TQ = 512
ATTN_HEADS_PER_STEP = 8
SCORE_LOOKAHEAD = 4
SAMPLE_LOOKAHEAD = 4


def kernel(x_prompt, x_sample, cache_k, cache_v, state_conv, state_lru, norm_mix, norm_mlp, norm_final, w_in, lambda_q, lambda_k, head_gain, conv_w, conv_b, w_rgate, b_rgate, w_igate, b_igate, lru_lambda, w_branch_attn, w_branch_lru, w_out, w_mlp_up, w_mlp_down):
    bp, sp, _ = x_prompt.shape
    bs, ss, _ = x_sample.shape
    depth = w_in.shape[0]
    assert depth == 1, "the final norm is fused into the last (only) layer's merge kernel"
    past = cache_k.shape[2]
    assert past % CHUNK == 0 and ss <= CHUNK

    l = 0
    lam_init = 0.8 - 0.6 * math.exp(-0.3 * l)
    p = _layer_params(l, norm_mix, norm_mlp, norm_final, w_in, lambda_q, lambda_k, head_gain, conv_w,
                      conv_b, w_rgate, b_rgate, w_igate, b_igate, lru_lambda, w_branch_attn,
                      w_branch_lru, w_out, w_mlp_up, w_mlp_down)

    xp2 = x_prompt.reshape(bp * sp, D_MODEL)
    conv0_p = jnp.zeros((CONV_WIDTH - 1, bp, LRU_WIDTH), F32)
    h0_p = jnp.zeros((bp, LRU_WIDTH), F32)
    qt_p, kt_p, v_p, kb_p, vbt_p, ga_p, lru_p, conv_p, hl_p = _proj_call(
        xp2, conv0_p, h0_p, p, seq_len=sp, tm=PROJ_TM, feature_major=True)
    o_p = _attn_prompt_call(qt_p, kb_p, vbt_p, p, batch=bp, seq_len=sp, tq=ATTN_TQ,
                            hp=ATTN_HEADS_PER_STEP, lam_init=lam_init)
    y_p = _merge_call(xp2, o_p, ga_p, lru_p, p, tm=MERGE_TM)

    xs2 = x_sample.reshape(bs * ss, D_MODEL)
    conv0_s = jnp.transpose(state_conv[l], (1, 0, 2))
    q_s, k_s, v_s, kb_s, vb_s, ga_s, lru_s, conv_s, hl_s = _proj_call(
        xs2, conv0_s, state_lru[l], p, seq_len=ss, tm=PROJ_TM, feature_major=False)
    ckt = jnp.transpose(cache_k[l], (0, 2, 3, 4, 1)).reshape(bs, N_HEADS, V_DIM, past)
    cv = cache_v[l].reshape(bs, past * N_HEADS, V_DIM)
    o_s = _attn_sample_call(q_s, kb_s, vb_s, ckt, cv, p, batch=bs, seq_len=ss, lam_init=lam_init)
    y_s = _merge_call(xs2, o_s, ga_s, lru_s, p, tm=MERGE_TM)

    k_prompt = jnp.transpose(kt_p.reshape(bp, N_HEADS, 2, HEAD_DIM, sp), (0, 4, 1, 2, 3))
    return (
        y_p.reshape(bp, sp, D_MODEL),
        y_s.reshape(bs, ss, D_MODEL),
        k_prompt[None],
        v_p.reshape(1, bp, sp, N_HEADS, V_DIM),
        jnp.transpose(conv_p, (1, 0, 2))[None],
        hl_p[None],
        k_s.reshape(1, bs, ss, N_HEADS, 2, HEAD_DIM),
        v_s.reshape(1, bs, ss, N_HEADS, V_DIM),
        jnp.transpose(conv_s, (1, 0, 2))[None],
        hl_s[None],
    )
```

```python
import functools
import math

import jax
import jax.numpy as jnp
import numpy as np
from jax import lax
from jax.experimental import pallas as pl
from jax.experimental.pallas import tpu as pltpu

D_MODEL = 1024
N_HEADS = 8
HEAD_DIM = 64
V_DIM = 2 * HEAD_DIM
CHUNK = 64
LRU_WIDTH = D_MODEL
LRU_BLOCKS = 16
LRU_BLOCK = LRU_WIDTH // LRU_BLOCKS
CONV_WIDTH = 4
LRU_C = 8.0
D_FF = 4 * D_MODEL
EPS = 1e-6

SUBLANES = 8
LANES = 128
MXU_DEPTH = 256
GATE_GROUP = MXU_DEPTH // LRU_BLOCK
N_GATE_GROUPS = LRU_BLOCKS // GATE_GROUP
CONV_PAD = SUBLANES
MASK_VALUE = -0.7 * float(np.finfo(np.float32).max)
Q_SCALE_LOG2 = HEAD_DIM ** -0.5 * math.log2(math.e)
VMEM_LIMIT_BYTES = 56 * 1024 * 1024

PROJ_COLUMNS = ("q", "k", "v", "x_lru", "g_lru", "gate_a", "gate_b")
PARKED = ("q", "k", "v", "g_lru", "gate_a", "gate_b")

F32 = jnp.float32
BF16 = jnp.bfloat16
NT_DIMS = (((1,), (1,)), ((), ()))


def _rms_norm_f32(x, g):
    inv = lax.rsqrt(jnp.sum(x * x, axis=-1, keepdims=True) * (1.0 / x.shape[-1]) + EPS)
    return (x * inv) * g


def _const_spec(shape):
    zeros = (0,) * len(shape)
    return pl.BlockSpec(shape, lambda *_: zeros, pipeline_mode=pl.Buffered(1))


def _store_head_rows(v_ref, val, rows):
    for h in range(N_HEADS):
        v_ref[pl.ds(h, rows, stride=N_HEADS), :] = val[:, h * V_DIM:(h + 1) * V_DIM]


def _proj_kernel(x_ref, conv0_ref, h0_ref, gmix_ref, win_ref, wg_ref, bg_ref, convw_ref,
                 convb_ref, lam_ref, wbl_ref,
                 q_ref, k_ref, v_ref, kb_ref, vb_ref, ga_ref, lru_ref, convout_ref, hout_ref,
                 xn_scr, xpad_scr, hc_scr, raw_scr,
                 *, tm, seg_len, tiles_per_seq, feature_major):
    nseg = tm // seg_len
    i = pl.program_id(0)
    hist0 = CONV_PAD - (CONV_WIDTH - 1)
    tail0 = CONV_PAD + seg_len - (CONV_WIDTH - 1)

    if tiles_per_seq > 1:
        @pl.when(i == 0)
        def _():
            xpad_scr[:, 0:CONV_PAD, :] = jnp.zeros((nseg, CONV_PAD, LRU_WIDTH), F32)
            hc_scr[...] = jnp.zeros(hc_scr.shape, F32)

    x = x_ref[...]
    xn_scr[...] = _rms_norm_f32(x, gmix_ref[...]).astype(BF16)

    def proj(c):
        return jnp.dot(xn_scr[...], win_ref[:, c * D_MODEL:(c + 1) * D_MODEL],
                       preferred_element_type=F32)

    def park(name):
        raw_scr[PARKED.index(name)] = proj(PROJ_COLUMNS.index(name))

    def parked(name):
        return raw_scr[PARKED.index(name)]

    xl = proj(PROJ_COLUMNS.index("x_lru"))
    for j in range(nseg):
        xpad_scr[j, CONV_PAD:CONV_PAD + seg_len, :] = xl[j * seg_len:(j + 1) * seg_len]
    park("q")
    park("k")

    def stream(j):
        return i // tiles_per_seq if tiles_per_seq > 1 else i * nseg + j

    stream_start = (i % tiles_per_seq) == 0
    for j in range(nseg):
        for t in range(CONV_WIDTH - 1):
            rows = slice(hist0 + t, hist0 + t + 1)
            initial = conv0_ref[t, pl.ds(stream(j), 1), :]
            if tiles_per_seq > 1:
                initial = jnp.where(stream_start, initial, xpad_scr[j, rows, :])
            xpad_scr[j, rows, :] = initial

    convw = convw_ref[...]
    xc_parts = []
    for j in range(nseg):
        acc = convb_ref[...] + convw[0:1, :] * xpad_scr[j, pl.ds(hist0, seg_len), :]
        for t in range(1, CONV_WIDTH):
            acc = acc + convw[t:t + 1, :] * xpad_scr[j, pl.ds(hist0 + t, seg_len), :]
        xc_parts.append(acc)
    xc = xc_parts[0] if nseg == 1 else jnp.concatenate(xc_parts, axis=0)

    xcb = xc.astype(BF16)

    def gate(which):
        cols = [jnp.dot(xcb[:, g * MXU_DEPTH:(g + 1) * MXU_DEPTH], wg_ref[which, g],
                        preferred_element_type=F32) for g in range(N_GATE_GROUPS)]
        return jax.nn.sigmoid(jnp.concatenate(cols, axis=1) + bg_ref[which:which + 1, :])

    r = gate(0)
    ig = gate(1)

    for name in ("v", "g_lru", "gate_a", "gate_b"):
        park(name)
    qq = parked("q") * Q_SCALE_LOG2
    kk = parked("k")
    vv = parked("v")
    kb_ref[...] = kk.astype(BF16)
    _store_head_rows(v_ref, vv, tm)
    if feature_major:
        q_ref[0] = qq.T.astype(BF16)
        k_ref[0] = kk.T
        vb_ref[0] = vv.T.astype(BF16)
    else:
        q_ref[...] = qq.astype(BF16)
        k_ref[...] = kk
        vb_ref[...] = vv.astype(BF16)
    ga_ref[...] = jax.nn.sigmoid(parked("gate_a"))
    sig_b = jax.nn.sigmoid(parked("gate_b"))
    gelu_g = jax.nn.gelu(parked("g_lru"))

    z = -lam_ref[...]
    softplus = jnp.maximum(z, 0.0) + jnp.log1p(jnp.exp(-jnp.abs(z)))
    log_a = (-LRU_C * r) * softplus
    a = jnp.exp(log_a)
    mult = jnp.sqrt(-jnp.tanh(log_a) * (a * a + 1.0))
    u = mult * (ig * xc)

    groups = tm // SUBLANES
    a3 = a.reshape(groups, SUBLANES, LRU_WIDTH)
    u3 = u.reshape(groups, SUBLANES, LRU_WIDTH)
    row = lax.broadcasted_iota(jnp.int32, (1, SUBLANES, LRU_WIDTH), 1)
    shift = 1
    while shift < SUBLANES:
        a_prev = pltpu.roll(a3, shift, axis=1)
        u_prev = pltpu.roll(u3, shift, axis=1)
        keep = row >= shift
        u3 = jnp.where(keep, a3 * u_prev + u3, u3)
        a3 = jnp.where(keep, a3 * a_prev, a3)
        shift *= 2
    seg_groups = seg_len // SUBLANES
    h_groups = []
    for j in range(nseg):
        b = stream(j)
        h_prev = h0_ref[pl.ds(b, 1), :]
        if tiles_per_seq > 1:
            h_prev = jnp.where(stream_start, h_prev, hc_scr[0:1, :])
        for g in range(j * seg_groups, (j + 1) * seg_groups):
            hg = u3[g] + a3[g] * h_prev
            h_groups.append(hg)
            h_prev = hg[SUBLANES - 1:SUBLANES, :]
        hout_ref[pl.ds(b, 1), :] = h_prev
        for t in range(CONV_WIDTH - 1):
            convout_ref[t, pl.ds(b, 1), :] = xpad_scr[j, tail0 + t:tail0 + t + 1, :]
        if tiles_per_seq > 1:
            hc_scr[0:1, :] = h_prev
            xpad_scr[j, hist0:CONV_PAD, :] = xpad_scr[j, tail0:tail0 + CONV_WIDTH - 1, :]
    h = jnp.concatenate(h_groups, axis=0)

    y = (h * gelu_g).astype(BF16)
    lru_ref[...] = sig_b * jnp.dot(y, wbl_ref[...], preferred_element_type=F32)


def _proj_call(x2d, conv0, h0, p, *, seq_len, tm, feature_major):
    n = x2d.shape[0]
    nb = n // seq_len
    seg_len = min(seq_len, tm)
    nseg = tm // seg_len
    tiles_per_seq = seq_len // seg_len
    assert n % tm == 0 and tm % seg_len == 0 and seq_len % seg_len == 0 and seg_len % SUBLANES == 0
    assert nseg == 1 or tiles_per_seq == 1

    row_map = lambda i: (i, 0)
    tok_f32 = jax.ShapeDtypeStruct((n, D_MODEL), F32)
    tok_bf16 = jax.ShapeDtypeStruct((n, D_MODEL), BF16)
    tok_spec = pl.BlockSpec((tm, D_MODEL), row_map)
    if feature_major:
        fm_spec = pl.BlockSpec((1, D_MODEL, tm), lambda i: (i // tiles_per_seq, 0, i % tiles_per_seq))
        fm_f32 = jax.ShapeDtypeStruct((nb, D_MODEL, seq_len), F32)
        fm_bf16 = jax.ShapeDtypeStruct((nb, D_MODEL, seq_len), BF16)
    else:
        fm_spec, fm_f32, fm_bf16 = tok_spec, tok_f32, tok_bf16
    kernel = functools.partial(_proj_kernel, tm=tm, seg_len=seg_len, tiles_per_seq=tiles_per_seq,
                               feature_major=feature_major)
    return pl.pallas_call(
        kernel,
        grid=(n // tm,),
        in_specs=[
            pl.BlockSpec((tm, D_MODEL), row_map),
            _const_spec(conv0.shape),
            _const_spec(h0.shape),
            _const_spec((1, D_MODEL)),
            _const_spec(p["w_in"].shape),
            _const_spec(p["w_gate"].shape),
            _const_spec((2, LRU_WIDTH)),
            _const_spec((CONV_WIDTH, LRU_WIDTH)),
            _const_spec((1, LRU_WIDTH)),
            _const_spec((1, LRU_WIDTH)),
            _const_spec((LRU_WIDTH, D_MODEL)),
        ],
        out_specs=[
            fm_spec,
            fm_spec,
            pl.BlockSpec((tm * N_HEADS, V_DIM), row_map),
            tok_spec,
            fm_spec,
            tok_spec,
            tok_spec,
            pl.BlockSpec(conv0.shape, lambda i: (0, 0, 0)),
            pl.BlockSpec(h0.shape, lambda i: (0, 0)),
        ],
        out_shape=[fm_bf16, fm_f32, jax.ShapeDtypeStruct((n * N_HEADS, V_DIM), F32), tok_bf16,
                   fm_bf16, tok_f32, tok_f32,
                   jax.ShapeDtypeStruct(conv0.shape, F32), jax.ShapeDtypeStruct(h0.shape, F32)],
        scratch_shapes=[
            pltpu.VMEM((tm, D_MODEL), BF16),
            pltpu.VMEM((nseg, CONV_PAD + seg_len, LRU_WIDTH), F32),
            pltpu.VMEM((SUBLANES, LRU_WIDTH), F32),
            pltpu.VMEM((len(PARKED), tm, D_MODEL), F32),
        ],
        compiler_params=pltpu.CompilerParams(
            dimension_semantics=("arbitrary",), vmem_limit_bytes=VMEM_LIMIT_BYTES),
        name="proj_lru",
    )(x2d, conv0, h0, p["g_mix"], p["w_in"], p["w_gate"], p["b_gate"], p["conv_w"],
      p["conv_b"], p["lru_lambda"], p["w_bl"])


def _split_maps(q, tq, q2_scr):
    lane = lax.broadcasted_iota(jnp.int32, q.shape, 1)
    zero = jnp.zeros_like(q)
    q2_scr[0:tq, :] = jnp.where(lane < HEAD_DIM, q, zero)
    q2_scr[tq:2 * tq, :] = jnp.where(lane >= HEAD_DIM, q, zero)


def _lambda_full(lq_ref, lk_ref, lam_init):
    e0 = jnp.exp(jnp.sum(lq_ref[0:1, :] * lk_ref[0:1, :], axis=1, keepdims=True))
    e1 = jnp.exp(jnp.sum(lq_ref[1:2, :] * lk_ref[1:2, :], axis=1, keepdims=True))
    return e0 - e1 + lam_init


def _finish_head(acc, l_row, tq, lam, gain, lam_init):
    o1 = acc[0:tq] / l_row[0:tq]
    o2 = acc[tq:2 * tq] / l_row[tq:2 * tq]
    o = o1 - lam * o2
    o = o * lax.rsqrt(jnp.mean(o * o, axis=-1, keepdims=True) + EPS)
    return o * gain * (1.0 - lam_init)


def _online_step_t(s_ref, m8, vt, m_scr, l_scr, acc_scr):
    keys, cols = s_ref.shape
    m_prev = m_scr[...]
    m_new = jnp.maximum(m_prev, jnp.max(m8, axis=0, keepdims=True))
    alpha = jnp.exp2(m_prev - m_new)
    p3 = jnp.exp2(s_ref[...].reshape(keys // SUBLANES, SUBLANES, cols) - m_new)
    l_scr[...] = alpha * l_scr[...] + jnp.sum(p3, axis=0)
    p = p3.reshape(keys, cols).astype(BF16)
    acc_scr[...] = alpha * acc_scr[...] + jnp.dot(vt, p, preferred_element_type=F32)
    m_scr[...] = m_new


def _attn_prompt_kernel(lq_ref, lk_ref, gain_ref, qt_ref, k_ref, vt_ref, o_ref,
                        q2t_scr, s_scr, m_scr, l_scr, acc_scr, *, tq, hp, lam_init):
    qi = pl.program_id(2)
    feat = lax.broadcasted_iota(jnp.int32, (V_DIM, tq), 0)
    for g in range(hp):
        qt = qt_ref[0, g * V_DIM:(g + 1) * V_DIM, :]
        zero = jnp.zeros_like(qt)
        q2t_scr[g, 0] = jnp.where(feat < HEAD_DIM, qt, zero)
        q2t_scr[g, 1] = jnp.where(feat >= HEAD_DIM, qt, zero)
    m_scr[...] = jnp.full(m_scr.shape, MASK_VALUE, F32)
    l_scr[...] = jnp.zeros(l_scr.shape, F32)
    acc_scr[...] = jnp.zeros(acc_scr.shape, F32)
    chains = [(g, c) for g in range(hp) for c in range(2)]

    def run(units):
        def scores(unit, slot):
            g, c, k0, kl, c0, cl, visible = unit
            st = jnp.dot(k_ref[pl.ds(k0, kl), g * V_DIM:(g + 1) * V_DIM], q2t_scr[g, c, :, c0:c0 + cl],
                         preferred_element_type=F32)
            if visible is not None:
                st = jnp.where(visible, st, MASK_VALUE)
            s_scr[slot, 0:kl, 0:cl] = st
            return jnp.max(st.reshape(kl // SUBLANES, SUBLANES, cl), axis=0)

        nslot = SCORE_LOOKAHEAD + 1
        m8s = [scores(u, n % nslot) for n, u in enumerate(units[:SCORE_LOOKAHEAD])]
        for n, (g, c, k0, kl, c0, cl, _) in enumerate(units):
            vt = vt_ref[0, g * V_DIM:(g + 1) * V_DIM, pl.ds(k0, kl)]
            cols = slice(c0, c0 + cl)
            _online_step_t(s_scr.at[n % nslot, 0:kl, 0:cl], m8s[n], vt, m_scr.at[g, c, :, cols],
                           l_scr.at[g, c, :, cols], acc_scr.at[g, c, :, cols])
            if n + SCORE_LOOKAHEAD < len(units):
                m8s.append(scores(units[n + SCORE_LOOKAHEAD], (n + SCORE_LOOKAHEAD) % nslot))

    def full_tile(j, carry):
        k0 = pl.multiple_of(j * tq, tq)
        run([(g, c, k0, tq, 0, tq, None) for g, c in chains])
        return carry

    lax.fori_loop(0, qi, full_tile, 0)

    half = tq // 2
    d0 = pl.multiple_of(qi * tq, tq)
    d1 = pl.multiple_of(qi * tq + half, half)
    k_chunk = lax.broadcasted_iota(jnp.int32, (half, tq), 0) // CHUNK
    q_chunk = lax.broadcasted_iota(jnp.int32, (half, tq), 1) // CHUNK
    first_keys = k_chunk <= q_chunk
    second_keys = first_keys[:, 0:half]
    run([(g, c, d0, half, 0, tq, first_keys) for g, c in chains]
        + [(g, c, d1, half, half, half, second_keys) for g, c in chains])

    lam = _lambda_full(lq_ref, lk_ref, lam_init)
    for g in range(hp):
        o1 = acc_scr[g, 0] / jnp.sum(l_scr[g, 0], axis=0, keepdims=True)
        o2 = acc_scr[g, 1] / jnp.sum(l_scr[g, 1], axis=0, keepdims=True)
        ot = o1 - lam * o2
        ot = ot * lax.rsqrt(jnp.mean(ot * ot, axis=0, keepdims=True) + EPS)
        ot = ot * gain_ref[...] * (1.0 - lam_init)
        o_ref[:, g * V_DIM:(g + 1) * V_DIM] = ot.T.astype(o_ref.dtype)


def _attn_prompt_call(qt, kb, vbt, p, *, batch, seq_len, tq, hp, lam_init):
    assert seq_len % tq == 0 and tq % CHUNK == 0 and tq % LANES == 0 and N_HEADS % hp == 0
    nq = seq_len // tq
    kernel = functools.partial(_attn_prompt_kernel, tq=tq, hp=hp, lam_init=lam_init)
    return pl.pallas_call(
        kernel,
        grid=(batch, N_HEADS // hp, nq),
        in_specs=[
            _const_spec((2, HEAD_DIM)),
            _const_spec((2, HEAD_DIM)),
            _const_spec((V_DIM, 1)),
            pl.BlockSpec((1, hp * V_DIM, tq), lambda b, h, i: (b, h, i)),
            pl.BlockSpec((seq_len, hp * V_DIM), lambda b, h, i: (b, h), pipeline_mode=pl.Buffered(1)),
            pl.BlockSpec((1, hp * V_DIM, seq_len), lambda b, h, i: (b, h, 0), pipeline_mode=pl.Buffered(1)),
        ],
        out_specs=pl.BlockSpec((tq, hp * V_DIM), lambda b, h, i: (b * nq + i, h)),
        out_shape=jax.ShapeDtypeStruct(kb.shape, BF16),
        scratch_shapes=[
            pltpu.VMEM((hp, 2, V_DIM, tq), BF16),
            pltpu.VMEM((SCORE_LOOKAHEAD + 1, tq, tq), F32),
            pltpu.VMEM((hp, 2, 1, tq), F32),
            pltpu.VMEM((hp, 2, SUBLANES, tq), F32),
            pltpu.VMEM((hp, 2, V_DIM, tq), F32),
        ],
        compiler_params=pltpu.CompilerParams(
            dimension_semantics=("arbitrary", "arbitrary", "arbitrary"),
            vmem_limit_bytes=VMEM_LIMIT_BYTES),
        name="attn_prompt",
    )(p["lambda_q"], p["lambda_k"], p["head_gain_col"], qt, kb, vbt)


def _attn_sample_kernel(lq_ref, lk_ref, gain_ref, q_ref, kn_ref, vn_ref, kct_ref, vc_ref, o_ref,
                        q2_scr, *, tq, past, lam_init):
    lam = _lambda_full(lq_ref, lk_ref, lam_init)

    def scores(h):
        cols = slice(h * V_DIM, (h + 1) * V_DIM)
        _split_maps(q_ref[:, cols], tq, q2_scr.at[h])
        q2 = q2_scr[h]
        s_c = jnp.dot(q2, kct_ref[0, h].astype(BF16), preferred_element_type=F32)
        s_n = lax.dot_general(q2, kn_ref[:, cols], NT_DIMS, preferred_element_type=F32)
        return s_c, s_n

    ahead = [scores(h) for h in range(SAMPLE_LOOKAHEAD)]
    for h in range(N_HEADS):
        cols = slice(h * V_DIM, (h + 1) * V_DIM)
        s_c, s_n = ahead[h]
        if h + SAMPLE_LOOKAHEAD < N_HEADS:
            ahead.append(scores(h + SAMPLE_LOOKAHEAD))
        m = jnp.maximum(jnp.max(s_c, axis=1, keepdims=True), jnp.max(s_n, axis=1, keepdims=True))
        p_c = jnp.exp2(s_c - m)
        p_n = jnp.exp2(s_n - m)
        l_row = jnp.sum(p_c, axis=1, keepdims=True) + jnp.sum(p_n, axis=1, keepdims=True)
        v_c = vc_ref[0, pl.ds(h, past, stride=N_HEADS), :].astype(BF16)
        acc = (jnp.dot(p_c.astype(BF16), v_c, preferred_element_type=F32)
               + jnp.dot(p_n.astype(BF16), vn_ref[:, cols], preferred_element_type=F32))
        o_ref[:, cols] = _finish_head(acc, l_row, tq, lam, gain_ref[...], lam_init).astype(o_ref.dtype)


def _attn_sample_call(q, kb, vb, cache_kt, cache_v, p, *, batch, seq_len, lam_init):
    past = cache_kt.shape[3]
    kernel = functools.partial(_attn_sample_kernel, tq=seq_len, past=past, lam_init=lam_init)
    return pl.pallas_call(
        kernel,
        grid=(batch,),
        in_specs=[
            _const_spec((2, HEAD_DIM)),
            _const_spec((2, HEAD_DIM)),
            _const_spec((1, V_DIM)),
            pl.BlockSpec((seq_len, D_MODEL), lambda b: (b, 0)),
            pl.BlockSpec((seq_len, D_MODEL), lambda b: (b, 0)),
            pl.BlockSpec((seq_len, D_MODEL), lambda b: (b, 0)),
            pl.BlockSpec((1, N_HEADS, V_DIM, past), lambda b: (b, 0, 0, 0)),
            pl.BlockSpec((1, past * N_HEADS, V_DIM), lambda b: (b, 0, 0)),
        ],
        out_specs=pl.BlockSpec((seq_len, D_MODEL), lambda b: (b, 0)),
        out_shape=jax.ShapeDtypeStruct(q.shape, BF16),
        scratch_shapes=[pltpu.VMEM((N_HEADS, 2 * seq_len, V_DIM), BF16)],
        compiler_params=pltpu.CompilerParams(
            dimension_semantics=("arbitrary",), vmem_limit_bytes=VMEM_LIMIT_BYTES),
        name="attn_sample",
    )(p["lambda_q"], p["lambda_k"], p["head_gain"], q, kb, vb, cache_kt, cache_v)


def _merge_kernel(x_ref, o_ref, ga_ref, lru_ref, wba_ref, wo_ref, gmlp_ref, wup_ref, wdown_ref,
                  gfin_ref, y_ref):
    attn = jnp.dot(o_ref[...], wba_ref[...], preferred_element_type=F32)
    merged = ga_ref[...] * attn + lru_ref[...]
    h = x_ref[...] + jnp.dot(merged.astype(BF16), wo_ref[...], preferred_element_type=F32)
    hn = _rms_norm_f32(h, gmlp_ref[...]).astype(BF16)
    mlp = None
    for c in range(D_FF // D_MODEL):
        cols = slice(c * D_MODEL, (c + 1) * D_MODEL)
        up = jnp.dot(hn, wup_ref[:, cols], preferred_element_type=F32)
        act = jnp.square(jnp.maximum(up, 0.0)).astype(BF16)
        part = jnp.dot(act, wdown_ref[cols, :], preferred_element_type=F32)
        mlp = part if mlp is None else mlp + part
    y_ref[...] = _rms_norm_f32(h + mlp, gfin_ref[...])


def _merge_call(x2d, o_attn, ga, lru, p, *, tm):
    n = x2d.shape[0]
    assert n % tm == 0
    row_map = lambda i: (i, 0)
    return pl.pallas_call(
        _merge_kernel,
        grid=(n // tm,),
        in_specs=[
            pl.BlockSpec((tm, D_MODEL), row_map),
            pl.BlockSpec((tm, D_MODEL), row_map),
            pl.BlockSpec((tm, D_MODEL), row_map),
            pl.BlockSpec((tm, D_MODEL), row_map),
            _const_spec((D_MODEL, D_MODEL)),
            _const_spec((D_MODEL, D_MODEL)),
            _const_spec((1, D_MODEL)),
            _const_spec((D_MODEL, D_FF)),
            _const_spec((D_FF, D_MODEL)),
            _const_spec((1, D_MODEL)),
        ],
        out_specs=pl.BlockSpec((tm, D_MODEL), row_map),
        out_shape=jax.ShapeDtypeStruct((n, D_MODEL), F32),
        compiler_params=pltpu.CompilerParams(
            dimension_semantics=("arbitrary",), vmem_limit_bytes=VMEM_LIMIT_BYTES),
        name="merge_mlp",
    )(x2d, o_attn, ga, lru, p["w_ba"], p["w_o"], p["g_mlp"], p["w_up"], p["w_down"], p["g_final"])


def _block_diag_groups(w):
    w4 = w.reshape(N_GATE_GROUPS, GATE_GROUP, LRU_BLOCK, LRU_BLOCK)
    eye = jnp.eye(GATE_GROUP, dtype=w.dtype)
    return jnp.einsum("gaij,ab->gaibj", w4, eye).reshape(N_GATE_GROUPS, MXU_DEPTH, MXU_DEPTH)


def _layer_params(l, norm_mix, norm_mlp, norm_final, w_in, lambda_q, lambda_k, head_gain, conv_w,
                  conv_b, w_rgate, b_rgate, w_igate, b_igate, lru_lambda, w_branch_attn,
                  w_branch_lru, w_out, w_mlp_up, w_mlp_down):
    return {
        "g_mix": norm_mix[l][None, :],
        "g_mlp": norm_mlp[l][None, :],
        "g_final": norm_final[None, :],
        "w_in": w_in[l].astype(BF16),
        "head_gain_col": head_gain[l][:, None],
        "lambda_q": lambda_q[l],
        "lambda_k": lambda_k[l],
        "head_gain": head_gain[l][None, :],
        "conv_w": conv_w[l],
        "conv_b": conv_b[l][None, :],
        "w_gate": jnp.stack([_block_diag_groups(w_rgate[l]), _block_diag_groups(w_igate[l])]).astype(BF16),
        "b_gate": jnp.stack([b_rgate[l], b_igate[l]]),
        "lru_lambda": lru_lambda[l][None, :],
        "w_bl": w_branch_lru[l].astype(BF16),
        "w_ba": w_branch_attn[l].astype(BF16),
        "w_o": w_out[l].astype(BF16),
        "w_up": w_mlp_up[l].astype(BF16),
        "w_down": w_mlp_down[l].astype(BF16),
    }


PROJ_TM = 256
MERGE_TM = 512
ATTN_TQ = 512
ATTN_HEADS_PER_STEP = 8
SCORE_LOOKAHEAD = 4
SAMPLE_LOOKAHEAD = 4


def kernel(x_prompt, x_sample, cache_k, cache_v, state_conv, state_lru, norm_mix, norm_mlp, norm_final, w_in, lambda_q, lambda_k, head_gain, conv_w, conv_b, w_rgate, b_rgate, w_igate, b_igate, lru_lambda, w_branch_attn, w_branch_lru, w_out, w_mlp_up, w_mlp_down):
    bp, sp, _ = x_prompt.shape
    bs, ss, _ = x_sample.shape
    depth = w_in.shape[0]
    assert depth == 1, "the final norm is fused into the last (only) layer's merge kernel"
    past = cache_k.shape[2]
    assert past % CHUNK == 0 and ss <= CHUNK

    l = 0
    lam_init = 0.8 - 0.6 * math.exp(-0.3 * l)
    p = _layer_params(l, norm_mix, norm_mlp, norm_final, w_in, lambda_q, lambda_k, head_gain, conv_w,
                      conv_b, w_rgate, b_rgate, w_igate, b_igate, lru_lambda, w_branch_attn,
                      w_branch_lru, w_out, w_mlp_up, w_mlp_down)

    xp2 = x_prompt.reshape(bp * sp, D_MODEL)
    conv0_p = jnp.zeros((CONV_WIDTH - 1, bp, LRU_WIDTH), F32)
    h0_p = jnp.zeros((bp, LRU_WIDTH), F32)
    qt_p, kt_p, v_p, kb_p, vbt_p, ga_p, lru_p, conv_p, hl_p = _proj_call(
        xp2, conv0_p, h0_p, p, seq_len=sp, tm=PROJ_TM, feature_major=True)
    o_p = _attn_prompt_call(qt_p, kb_p, vbt_p, p, batch=bp, seq_len=sp, tq=ATTN_TQ,
                            hp=ATTN_HEADS_PER_STEP, lam_init=lam_init)
    y_p = _merge_call(xp2, o_p, ga_p, lru_p, p, tm=MERGE_TM)

    xs2 = x_sample.reshape(bs * ss, D_MODEL)
    conv0_s = jnp.transpose(state_conv[l], (1, 0, 2))
    q_s, k_s, v_s, kb_s, vb_s, ga_s, lru_s, conv_s, hl_s = _proj_call(
        xs2, conv0_s, state_lru[l], p, seq_len=ss, tm=PROJ_TM, feature_major=False)
    ckt = jnp.transpose(cache_k[l], (0, 2, 3, 4, 1)).reshape(bs, N_HEADS, V_DIM, past)
    cv = cache_v[l].reshape(bs, past * N_HEADS, V_DIM)
    o_s = _attn_sample_call(q_s, kb_s, vb_s, ckt, cv, p, batch=bs, seq_len=ss, lam_init=lam_init)
    y_s = _merge_call(xs2, o_s, ga_s, lru_s, p, tm=MERGE_TM)

    k_prompt = jnp.transpose(kt_p.reshape(bp, N_HEADS, 2, HEAD_DIM, sp), (0, 4, 1, 2, 3))
    return (
        y_p.reshape(bp, sp, D_MODEL),
        y_s.reshape(bs, ss, D_MODEL),
        k_prompt[None],
        v_p.reshape(1, bp, sp, N_HEADS, V_DIM),
        jnp.transpose(conv_p, (1, 0, 2))[None],
        hl_p[None],
        k_s.reshape(1, bs, ss, N_HEADS, 2, HEAD_DIM),
        v_s.reshape(1, bs, ss, N_HEADS, V_DIM),
        jnp.transpose(conv_s, (1, 0, 2))[None],
        hl_s[None],
    )
```

```python
import functools
import math

import jax
import jax.numpy as jnp
import numpy as np
from jax import lax
from jax.experimental import pallas as pl
from jax.experimental.pallas import tpu as pltpu

D_MODEL = 1024
N_HEADS = 8
HEAD_DIM = 64
V_DIM = 2 * HEAD_DIM
CHUNK = 64
LRU_WIDTH = D_MODEL
LRU_BLOCKS = 16
LRU_BLOCK = LRU_WIDTH // LRU_BLOCKS
CONV_WIDTH = 4
LRU_C = 8.0
D_FF = 4 * D_MODEL
EPS = 1e-6

SUBLANES = 8
LANES = 128
MXU_DEPTH = 256
GATE_GROUP = MXU_DEPTH // LRU_BLOCK
N_GATE_GROUPS = LRU_BLOCKS // GATE_GROUP
CONV_PAD = SUBLANES
MASK_VALUE = -0.7 * float(np.finfo(np.float32).max)
Q_SCALE_LOG2 = HEAD_DIM ** -0.5 * math.log2(math.e)
VMEM_LIMIT_BYTES = 56 * 1024 * 1024

PROJ_COLUMNS = ("q", "k", "v", "x_lru", "g_lru", "gate_a", "gate_b")
PARKED = ("q", "k", "v", "g_lru", "gate_a", "gate_b")

F32 = jnp.float32
BF16 = jnp.bfloat16
NT_DIMS = (((1,), (1,)), ((), ()))


def _rms_norm_f32(x, g):
    inv = lax.rsqrt(jnp.sum(x * x, axis=-1, keepdims=True) * (1.0 / x.shape[-1]) + EPS)
    return (x * inv) * g


def _const_spec(shape):
    zeros = (0,) * len(shape)
    return pl.BlockSpec(shape, lambda *_: zeros, pipeline_mode=pl.Buffered(1))


def _store_head_rows(v_ref, val, rows):
    for h in range(N_HEADS):
        v_ref[pl.ds(h, rows, stride=N_HEADS), :] = val[:, h * V_DIM:(h + 1) * V_DIM]


def _proj_kernel(x_ref, conv0_ref, h0_ref, gmix_ref, win_ref, wg_ref, bg_ref, convw_ref,
                 convb_ref, lam_ref, wbl_ref,
                 q_ref, k_ref, v_ref, kb_ref, vb_ref, ga_ref, lru_ref, convout_ref, hout_ref,
                 xn_scr, xpad_scr, hc_scr, raw_scr,
                 *, tm, seg_len, tiles_per_seq, feature_major):
    nseg = tm // seg_len
    i = pl.program_id(0)
    hist0 = CONV_PAD - (CONV_WIDTH - 1)
    tail0 = CONV_PAD + seg_len - (CONV_WIDTH - 1)

    if tiles_per_seq > 1:
        @pl.when(i == 0)
        def _():
            xpad_scr[:, 0:CONV_PAD, :] = jnp.zeros((nseg, CONV_PAD, LRU_WIDTH), F32)
            hc_scr[...] = jnp.zeros(hc_scr.shape, F32)

    x = x_ref[...]
    xn_scr[...] = _rms_norm_f32(x, gmix_ref[...]).astype(BF16)

    def proj(c):
        return jnp.dot(xn_scr[...], win_ref[:, c * D_MODEL:(c + 1) * D_MODEL],
                       preferred_element_type=F32)

    def park(name):
        raw_scr[PARKED.index(name)] = proj(PROJ_COLUMNS.index(name))

    def parked(name):
        return raw_scr[PARKED.index(name)]

    xl = proj(PROJ_COLUMNS.index("x_lru"))
    for j in range(nseg):
        xpad_scr[j, CONV_PAD:CONV_PAD + seg_len, :] = xl[j * seg_len:(j + 1) * seg_len]
    park("q")
    park("k")

    def stream(j):
        return i // tiles_per_seq if tiles_per_seq > 1 else i * nseg + j

    stream_start = (i % tiles_per_seq) == 0
    for j in range(nseg):
        for t in range(CONV_WIDTH - 1):
            rows = slice(hist0 + t, hist0 + t + 1)
            initial = conv0_ref[t, pl.ds(stream(j), 1), :]
            if tiles_per_seq > 1:
                initial = jnp.where(stream_start, initial, xpad_scr[j, rows, :])
            xpad_scr[j, rows, :] = initial

    convw = convw_ref[...]
    xc_parts = []
    for j in range(nseg):
        acc = convb_ref[...] + convw[0:1, :] * xpad_scr[j, pl.ds(hist0, seg_len), :]
        for t in range(1, CONV_WIDTH):
            acc = acc + convw[t:t + 1, :] * xpad_scr[j, pl.ds(hist0 + t, seg_len), :]
        xc_parts.append(acc)
    xc = xc_parts[0] if nseg == 1 else jnp.concatenate(xc_parts, axis=0)

    xcb = xc.astype(BF16)

    def gate(which):
        cols = [jnp.dot(xcb[:, g * MXU_DEPTH:(g + 1) * MXU_DEPTH], wg_ref[which, g],
                        preferred_element_type=F32) for g in range(N_GATE_GROUPS)]
        return jax.nn.sigmoid(jnp.concatenate(cols, axis=1) + bg_ref[which:which + 1, :])

    r = gate(0)
    ig = gate(1)

    for name in ("v", "g_lru", "gate_a", "gate_b"):
        park(name)
    qq = parked("q") * Q_SCALE_LOG2
    kk = parked("k")
    vv = parked("v")
    kb_ref[...] = kk.astype(BF16)
    _store_head_rows(v_ref, vv, tm)
    if feature_major:
        q_ref[0] = qq.T.astype(BF16)
        k_ref[0] = kk.T
        vb_ref[0] = vv.T.astype(BF16)
    else:
        q_ref[...] = qq.astype(BF16)
        k_ref[...] = kk
        vb_ref[...] = vv.astype(BF16)
    ga_ref[...] = jax.nn.sigmoid(parked("gate_a"))
    sig_b = jax.nn.sigmoid(parked("gate_b"))
    gelu_g = jax.nn.gelu(parked("g_lru"))

    z = -lam_ref[...]
    softplus = jnp.maximum(z, 0.0) + jnp.log1p(jnp.exp(-jnp.abs(z)))
    log_a = (-LRU_C * r) * softplus
    a = jnp.exp(log_a)
    mult = jnp.sqrt(-jnp.tanh(log_a) * (a * a + 1.0))
    u = mult * (ig * xc)

    groups = tm // SUBLANES
    a3 = a.reshape(groups, SUBLANES, LRU_WIDTH)
    u3 = u.reshape(groups, SUBLANES, LRU_WIDTH)
    row = lax.broadcasted_iota(jnp.int32, (1, SUBLANES, LRU_WIDTH), 1)
    shift = 1
    while shift < SUBLANES:
        a_prev = pltpu.roll(a3, shift, axis=1)
        u_prev = pltpu.roll(u3, shift, axis=1)
        keep = row >= shift
        u3 = jnp.where(keep, a3 * u_prev + u3, u3)
        a3 = jnp.where(keep, a3 * a_prev, a3)
        shift *= 2
    seg_groups = seg_len // SUBLANES
    h_groups = []
    for j in range(nseg):
        b = stream(j)
        h_prev = h0_ref[pl.ds(b, 1), :]
        if tiles_per_seq > 1:
            h_prev = jnp.where(stream_start, h_prev, hc_scr[0:1, :])
        for g in range(j * seg_groups, (j + 1) * seg_groups):
            hg = u3[g] + a3[g] * h_prev
            h_groups.append(hg)
            h_prev = hg[SUBLANES - 1:SUBLANES, :]
        hout_ref[pl.ds(b, 1), :] = h_prev
        for t in range(CONV_WIDTH - 1):
            convout_ref[t, pl.ds(b, 1), :] = xpad_scr[j, tail0 + t:tail0 + t + 1, :]
        if tiles_per_seq > 1:
            hc_scr[0:1, :] = h_prev
            xpad_scr[j, hist0:CONV_PAD, :] = xpad_scr[j, tail0:tail0 + CONV_WIDTH - 1, :]
    h = jnp.concatenate(h_groups, axis=0)

    y = (h * gelu_g).astype(BF16)
    lru_ref[...] = sig_b * jnp.dot(y, wbl_ref[...], preferred_element_type=F32)


def _proj_call(x2d, conv0, h0, p, *, seq_len, tm, feature_major):
    n = x2d.shape[0]
    nb = n // seq_len
    seg_len = min(seq_len, tm)
    nseg = tm // seg_len
    tiles_per_seq = seq_len // seg_len
    assert n % tm == 0 and tm % seg_len == 0 and seq_len % seg_len == 0 and seg_len % SUBLANES == 0
    assert nseg == 1 or tiles_per_seq == 1

    row_map = lambda i: (i, 0)
    tok_f32 = jax.ShapeDtypeStruct((n, D_MODEL), F32)
    tok_bf16 = jax.ShapeDtypeStruct((n, D_MODEL), BF16)
    tok_spec = pl.BlockSpec((tm, D_MODEL), row_map)
    if feature_major:
        fm_spec = pl.BlockSpec((1, D_MODEL, tm), lambda i: (i // tiles_per_seq, 0, i % tiles_per_seq))
        fm_f32 = jax.ShapeDtypeStruct((nb, D_MODEL, seq_len), F32)
        fm_bf16 = jax.ShapeDtypeStruct((nb, D_MODEL, seq_len), BF16)
    else:
        fm_spec, fm_f32, fm_bf16 = tok_spec, tok_f32, tok_bf16
    kernel = functools.partial(_proj_kernel, tm=tm, seg_len=seg_len, tiles_per_seq=tiles_per_seq,
                               feature_major=feature_major)
    return pl.pallas_call(
        kernel,
        grid=(n // tm,),
        in_specs=[
            pl.BlockSpec((tm, D_MODEL), row_map),
            _const_spec(conv0.shape),
            _const_spec(h0.shape),
            _const_spec((1, D_MODEL)),
            _const_spec(p["w_in"].shape),
            _const_spec(p["w_gate"].shape),
            _const_spec((2, LRU_WIDTH)),
            _const_spec((CONV_WIDTH, LRU_WIDTH)),
            _const_spec((1, LRU_WIDTH)),
            _const_spec((1, LRU_WIDTH)),
            _const_spec((LRU_WIDTH, D_MODEL)),
        ],
        out_specs=[
            fm_spec,
            fm_spec,
            pl.BlockSpec((tm * N_HEADS, V_DIM), row_map),
            tok_spec,
            fm_spec,
            tok_spec,
            tok_spec,
            pl.BlockSpec(conv0.shape, lambda i: (0, 0, 0)),
            pl.BlockSpec(h0.shape, lambda i: (0, 0)),
        ],
        out_shape=[fm_bf16, fm_f32, jax.ShapeDtypeStruct((n * N_HEADS, V_DIM), F32), tok_bf16,
                   fm_bf16, tok_f32, tok_f32,
                   jax.ShapeDtypeStruct(conv0.shape, F32), jax.ShapeDtypeStruct(h0.shape, F32)],
        scratch_shapes=[
            pltpu.VMEM((tm, D_MODEL), BF16),
            pltpu.VMEM((nseg, CONV_PAD + seg_len, LRU_WIDTH), F32),
            pltpu.VMEM((SUBLANES, LRU_WIDTH), F32),
            pltpu.VMEM((len(PARKED), tm, D_MODEL), F32),
        ],
        compiler_params=pltpu.CompilerParams(
            dimension_semantics=("arbitrary",), vmem_limit_bytes=VMEM_LIMIT_BYTES),
        name="proj_lru",
    )(x2d, conv0, h0, p["g_mix"], p["w_in"], p["w_gate"], p["b_gate"], p["conv_w"],
      p["conv_b"], p["lru_lambda"], p["w_bl"])


def _split_maps(q, tq, q2_scr):
    lane = lax.broadcasted_iota(jnp.int32, q.shape, 1)
    zero = jnp.zeros_like(q)
    q2_scr[0:tq, :] = jnp.where(lane < HEAD_DIM, q, zero)
    q2_scr[tq:2 * tq, :] = jnp.where(lane >= HEAD_DIM, q, zero)


def _lambda_full(lq_ref, lk_ref, lam_init):
    e0 = jnp.exp(jnp.sum(lq_ref[0:1, :] * lk_ref[0:1, :], axis=1, keepdims=True))
    e1 = jnp.exp(jnp.sum(lq_ref[1:2, :] * lk_ref[1:2, :], axis=1, keepdims=True))
    return e0 - e1 + lam_init


def _finish_head(acc, l_row, tq, lam, gain, lam_init):
    o1 = acc[0:tq] / l_row[0:tq]
    o2 = acc[tq:2 * tq] / l_row[tq:2 * tq]
    o = o1 - lam * o2
    o = o * lax.rsqrt(jnp.mean(o * o, axis=-1, keepdims=True) + EPS)
    return o * gain * (1.0 - lam_init)


def _online_step_t(s_ref, m8, vt, m_scr, l_scr, acc_scr):
    keys, cols = s_ref.shape
    m_prev = m_scr[...]
    m_new = jnp.maximum(m_prev, jnp.max(m8, axis=0, keepdims=True))
    alpha = jnp.exp2(m_prev - m_new)
    p3 = jnp.exp2(s_ref[...].reshape(keys // SUBLANES, SUBLANES, cols) - m_new)
    l_scr[...] = alpha * l_scr[...] + jnp.sum(p3, axis=0)
    p = p3.reshape(keys, cols).astype(BF16)
    acc_scr[...] = alpha * acc_scr[...] + jnp.dot(vt, p, preferred_element_type=F32)
    m_scr[...] = m_new


def _attn_prompt_kernel(lq_ref, lk_ref, gain_ref, qt_ref, k_ref, vt_ref, o_ref,
                        q2t_scr, s_scr, m_scr, l_scr, acc_scr, *, tq, hp, lam_init):
    qi = pl.program_id(2)
    feat = lax.broadcasted_iota(jnp.int32, (V_DIM, tq), 0)
    for g in range(hp):
        qt = qt_ref[0, g * V_DIM:(g + 1) * V_DIM, :]
        zero = jnp.zeros_like(qt)
        q2t_scr[g, 0] = jnp.where(feat < HEAD_DIM, qt, zero)
        q2t_scr[g, 1] = jnp.where(feat >= HEAD_DIM, qt, zero)
    m_scr[...] = jnp.full(m_scr.shape, MASK_VALUE, F32)
    l_scr[...] = jnp.zeros(l_scr.shape, F32)
    acc_scr[...] = jnp.zeros(acc_scr.shape, F32)
    chains = [(g, c) for g in range(hp) for c in range(2)]

    nslot = s_scr.shape[0]
    assert nslot > SCORE_LOOKAHEAD and len(chains) % nslot == 0

    def scores(unit, slot):
        g, c, k0, kl, c0, cl, visible = unit
        st = jnp.dot(k_ref[pl.ds(k0, kl), g * V_DIM:(g + 1) * V_DIM], q2t_scr[g, c, :, c0:c0 + cl],
                     preferred_element_type=F32)
        if visible is not None:
            st = jnp.where(visible, st, MASK_VALUE)
        s_scr[slot, 0:kl, 0:cl] = st
        return jnp.max(st.reshape(kl // SUBLANES, SUBLANES, cl), axis=0)

    def run(units, ready=(), then=()):
        upcoming = list(units) + list(then)
        m8s = list(ready)
        m8s += [scores(upcoming[n], n % nslot) for n in range(len(m8s), SCORE_LOOKAHEAD)]
        for n, (g, c, k0, kl, c0, cl, _) in enumerate(units):
            vt = vt_ref[0, g * V_DIM:(g + 1) * V_DIM, pl.ds(k0, kl)]
            cols = slice(c0, c0 + cl)
            _online_step_t(s_scr.at[n % nslot, 0:kl, 0:cl], m8s[n], vt, m_scr.at[g, c, :, cols],
                           l_scr.at[g, c, :, cols], acc_scr.at[g, c, :, cols])
            if n + SCORE_LOOKAHEAD < len(upcoming):
                m8s.append(scores(upcoming[n + SCORE_LOOKAHEAD], (n + SCORE_LOOKAHEAD) % nslot))
        return tuple(m8s[len(units):])

    def tile_units(j):
        k0 = pl.multiple_of(j * tq, tq)
        return [(g, c, k0, tq, 0, tq, None) for g, c in chains]

    def full_tile(j, ready):
        return run(tile_units(j), ready=ready, then=tile_units(j + 1)[:SCORE_LOOKAHEAD])

    first = tuple(scores(u, n) for n, u in enumerate(tile_units(0)[:SCORE_LOOKAHEAD]))
    lax.fori_loop(0, qi, full_tile, first)

    half = tq // 2
    d0 = pl.multiple_of(qi * tq, tq)
    d1 = pl.multiple_of(qi * tq + half, half)
    k_chunk = lax.broadcasted_iota(jnp.int32, (half, tq), 0) // CHUNK
    q_chunk = lax.broadcasted_iota(jnp.int32, (half, tq), 1) // CHUNK
    first_keys = k_chunk <= q_chunk
    second_keys = first_keys[:, 0:half]
    run([(g, c, d0, half, 0, tq, first_keys) for g, c in chains]
        + [(g, c, d1, half, half, half, second_keys) for g, c in chains])

    lam = _lambda_full(lq_ref, lk_ref, lam_init)
    for g in range(hp):
        o1 = acc_scr[g, 0] / jnp.sum(l_scr[g, 0], axis=0, keepdims=True)
        o2 = acc_scr[g, 1] / jnp.sum(l_scr[g, 1], axis=0, keepdims=True)
        ot = o1 - lam * o2
        ot = ot * lax.rsqrt(jnp.mean(ot * ot, axis=0, keepdims=True) + EPS)
        ot = ot * gain_ref[...] * (1.0 - lam_init)
        o_ref[:, g * V_DIM:(g + 1) * V_DIM] = ot.T.astype(o_ref.dtype)


def _attn_prompt_call(qt, kb, vbt, p, *, batch, seq_len, tq, hp, lam_init):
    assert seq_len % tq == 0 and tq % CHUNK == 0 and tq % LANES == 0 and N_HEADS % hp == 0
    nq = seq_len // tq
    kernel = functools.partial(_attn_prompt_kernel, tq=tq, hp=hp, lam_init=lam_init)
    return pl.pallas_call(
        kernel,
        grid=(batch, N_HEADS // hp, nq),
        in_specs=[
            _const_spec((2, HEAD_DIM)),
            _const_spec((2, HEAD_DIM)),
            _const_spec((V_DIM, 1)),
            pl.BlockSpec((1, hp * V_DIM, tq), lambda b, h, i: (b, h, i)),
            pl.BlockSpec((seq_len, hp * V_DIM), lambda b, h, i: (b, h), pipeline_mode=pl.Buffered(1)),
            pl.BlockSpec((1, hp * V_DIM, seq_len), lambda b, h, i: (b, h, 0), pipeline_mode=pl.Buffered(1)),
        ],
        out_specs=pl.BlockSpec((tq, hp * V_DIM), lambda b, h, i: (b * nq + i, h)),
        out_shape=jax.ShapeDtypeStruct(kb.shape, BF16),
        scratch_shapes=[
            pltpu.VMEM((hp, 2, V_DIM, tq), BF16),
            pltpu.VMEM((SCORE_RING_SLOTS, tq, tq), F32),
            pltpu.VMEM((hp, 2, 1, tq), F32),
            pltpu.VMEM((hp, 2, SUBLANES, tq), F32),
            pltpu.VMEM((hp, 2, V_DIM, tq), F32),
        ],
        compiler_params=pltpu.CompilerParams(
            dimension_semantics=("arbitrary", "arbitrary", "arbitrary"),
            vmem_limit_bytes=VMEM_LIMIT_BYTES),
        name="attn_prompt",
    )(p["lambda_q"], p["lambda_k"], p["head_gain_col"], qt, kb, vbt)


def _attn_sample_kernel(lq_ref, lk_ref, gain_ref, q_ref, kn_ref, vn_ref, kct_ref, vc_ref, o_ref,
                        q2_scr, *, tq, past, lam_init):
    lam = _lambda_full(lq_ref, lk_ref, lam_init)

    def scores(h):
        cols = slice(h * V_DIM, (h + 1) * V_DIM)
        _split_maps(q_ref[:, cols], tq, q2_scr.at[h])
        q2 = q2_scr[h]
        s_c = jnp.dot(q2, kct_ref[0, h].astype(BF16), preferred_element_type=F32)
        s_n = lax.dot_general(q2, kn_ref[:, cols], NT_DIMS, preferred_element_type=F32)
        return s_c, s_n

    ahead = [scores(h) for h in range(SAMPLE_LOOKAHEAD)]
    for h in range(N_HEADS):
        cols = slice(h * V_DIM, (h + 1) * V_DIM)
        s_c, s_n = ahead[h]
        if h + SAMPLE_LOOKAHEAD < N_HEADS:
            ahead.append(scores(h + SAMPLE_LOOKAHEAD))
        m = jnp.maximum(jnp.max(s_c, axis=1, keepdims=True), jnp.max(s_n, axis=1, keepdims=True))
        p_c = jnp.exp2(s_c - m)
        p_n = jnp.exp2(s_n - m)
        l_row = jnp.sum(p_c, axis=1, keepdims=True) + jnp.sum(p_n, axis=1, keepdims=True)
        v_c = vc_ref[0, pl.ds(h, past, stride=N_HEADS), :].astype(BF16)
        acc = (jnp.dot(p_c.astype(BF16), v_c, preferred_element_type=F32)
               + jnp.dot(p_n.astype(BF16), vn_ref[:, cols], preferred_element_type=F32))
        o_ref[:, cols] = _finish_head(acc, l_row, tq, lam, gain_ref[...], lam_init).astype(o_ref.dtype)


def _attn_sample_call(q, kb, vb, cache_kt, cache_v, p, *, batch, seq_len, lam_init):
    past = cache_kt.shape[3]
    kernel = functools.partial(_attn_sample_kernel, tq=seq_len, past=past, lam_init=lam_init)
    return pl.pallas_call(
        kernel,
        grid=(batch,),
        in_specs=[
            _const_spec((2, HEAD_DIM)),
            _const_spec((2, HEAD_DIM)),
            _const_spec((1, V_DIM)),
            pl.BlockSpec((seq_len, D_MODEL), lambda b: (b, 0)),
            pl.BlockSpec((seq_len, D_MODEL), lambda b: (b, 0)),
            pl.BlockSpec((seq_len, D_MODEL), lambda b: (b, 0)),
            pl.BlockSpec((1, N_HEADS, V_DIM, past), lambda b: (b, 0, 0, 0)),
            pl.BlockSpec((1, past * N_HEADS, V_DIM), lambda b: (b, 0, 0)),
        ],
        out_specs=pl.BlockSpec((seq_len, D_MODEL), lambda b: (b, 0)),
        out_shape=jax.ShapeDtypeStruct(q.shape, BF16),
        scratch_shapes=[pltpu.VMEM((N_HEADS, 2 * seq_len, V_DIM), BF16)],
        compiler_params=pltpu.CompilerParams(
            dimension_semantics=("arbitrary",), vmem_limit_bytes=VMEM_LIMIT_BYTES),
        name="attn_sample",
    )(p["lambda_q"], p["lambda_k"], p["head_gain"], q, kb, vb, cache_kt, cache_v)


def _merge_kernel(x_ref, o_ref, ga_ref, lru_ref, wba_ref, wo_ref, gmlp_ref, wup_ref, wdown_ref,
                  gfin_ref, y_ref):
    attn = jnp.dot(o_ref[...], wba_ref[...], preferred_element_type=F32)
    merged = ga_ref[...] * attn + lru_ref[...]
    h = x_ref[...] + jnp.dot(merged.astype(BF16), wo_ref[...], preferred_element_type=F32)
    hn = _rms_norm_f32(h, gmlp_ref[...]).astype(BF16)
    mlp = None
    for c in range(D_FF // D_MODEL):
        cols = slice(c * D_MODEL, (c + 1) * D_MODEL)
        up = jnp.dot(hn, wup_ref[:, cols], preferred_element_type=F32)
        act = jnp.square(jnp.maximum(up, 0.0)).astype(BF16)
        part = jnp.dot(act, wdown_ref[cols, :], preferred_element_type=F32)
        mlp = part if mlp is None else mlp + part
    y_ref[...] = _rms_norm_f32(h + mlp, gfin_ref[...])


def _merge_call(x2d, o_attn, ga, lru, p, *, tm):
    n = x2d.shape[0]
    assert n % tm == 0
    row_map = lambda i: (i, 0)
    return pl.pallas_call(
        _merge_kernel,
        grid=(n // tm,),
        in_specs=[
            pl.BlockSpec((tm, D_MODEL), row_map),
            pl.BlockSpec((tm, D_MODEL), row_map),
            pl.BlockSpec((tm, D_MODEL), row_map),
            pl.BlockSpec((tm, D_MODEL), row_map),
            _const_spec((D_MODEL, D_MODEL)),
            _const_spec((D_MODEL, D_MODEL)),
            _const_spec((1, D_MODEL)),
            _const_spec((D_MODEL, D_FF)),
            _const_spec((D_FF, D_MODEL)),
            _const_spec((1, D_MODEL)),
        ],
        out_specs=pl.BlockSpec((tm, D_MODEL), row_map),
        out_shape=jax.ShapeDtypeStruct((n, D_MODEL), F32),
        compiler_params=pltpu.CompilerParams(
            dimension_semantics=("arbitrary",), vmem_limit_bytes=VMEM_LIMIT_BYTES),
        name="merge_mlp",
    )(x2d, o_attn, ga, lru, p["w_ba"], p["w_o"], p["g_mlp"], p["w_up"], p["w_down"], p["g_final"])


def _block_diag_groups(w):
    w4 = w.reshape(N_GATE_GROUPS, GATE_GROUP, LRU_BLOCK, LRU_BLOCK)
    eye = jnp.eye(GATE_GROUP, dtype=w.dtype)
    return jnp.einsum("gaij,ab->gaibj", w4, eye).reshape(N_GATE_GROUPS, MXU_DEPTH, MXU_DEPTH)


def _layer_params(l, norm_mix, norm_mlp, norm_final, w_in, lambda_q, lambda_k, head_gain, conv_w,
                  conv_b, w_rgate, b_rgate, w_igate, b_igate, lru_lambda, w_branch_attn,
                  w_branch_lru, w_out, w_mlp_up, w_mlp_down):
    return {
        "g_mix": norm_mix[l][None, :],
        "g_mlp": norm_mlp[l][None, :],
        "g_final": norm_final[None, :],
        "w_in": w_in[l].astype(BF16),
        "head_gain_col": head_gain[l][:, None],
        "lambda_q": lambda_q[l],
        "lambda_k": lambda_k[l],
        "head_gain": head_gain[l][None, :],
        "conv_w": conv_w[l],
        "conv_b": conv_b[l][None, :],
        "w_gate": jnp.stack([_block_diag_groups(w_rgate[l]), _block_diag_groups(w_igate[l])]).astype(BF16),
        "b_gate": jnp.stack([b_rgate[l], b_igate[l]]),
        "lru_lambda": lru_lambda[l][None, :],
        "w_bl": w_branch_lru[l].astype(BF16),
        "w_ba": w_branch_attn[l].astype(BF16),
        "w_o": w_out[l].astype(BF16),
        "w_up": w_mlp_up[l].astype(BF16),
        "w_down": w_mlp_down[l].astype(BF16),
    }


PROJ_TM = 256
MERGE_TM = 512
ATTN_TQ = 512
ATTN_HEADS_PER_STEP = 8
SCORE_LOOKAHEAD = 4
SCORE_RING_SLOTS = 8
SAMPLE_LOOKAHEAD = 4


def kernel(x_prompt, x_sample, cache_k, cache_v, state_conv, state_lru, norm_mix, norm_mlp, norm_final, w_in, lambda_q, lambda_k, head_gain, conv_w, conv_b, w_rgate, b_rgate, w_igate, b_igate, lru_lambda, w_branch_attn, w_branch_lru, w_out, w_mlp_up, w_mlp_down):
    bp, sp, _ = x_prompt.shape
    bs, ss, _ = x_sample.shape
    depth = w_in.shape[0]
    assert depth == 1, "the final norm is fused into the last (only) layer's merge kernel"
    past = cache_k.shape[2]
    assert past % CHUNK == 0 and ss <= CHUNK

    l = 0
    lam_init = 0.8 - 0.6 * math.exp(-0.3 * l)
    p = _layer_params(l, norm_mix, norm_mlp, norm_final, w_in, lambda_q, lambda_k, head_gain, conv_w,
                      conv_b, w_rgate, b_rgate, w_igate, b_igate, lru_lambda, w_branch_attn,
                      w_branch_lru, w_out, w_mlp_up, w_mlp_down)

    xp2 = x_prompt.reshape(bp * sp, D_MODEL)
    conv0_p = jnp.zeros((CONV_WIDTH - 1, bp, LRU_WIDTH), F32)
    h0_p = jnp.zeros((bp, LRU_WIDTH), F32)
    qt_p, kt_p, v_p, kb_p, vbt_p, ga_p, lru_p, conv_p, hl_p = _proj_call(
        xp2, conv0_p, h0_p, p, seq_len=sp, tm=PROJ_TM, feature_major=True)
    o_p = _attn_prompt_call(qt_p, kb_p, vbt_p, p, batch=bp, seq_len=sp, tq=ATTN_TQ,
                            hp=ATTN_HEADS_PER_STEP, lam_init=lam_init)
    y_p = _merge_call(xp2, o_p, ga_p, lru_p, p, tm=MERGE_TM)

    xs2 = x_sample.reshape(bs * ss, D_MODEL)
    conv0_s = jnp.transpose(state_conv[l], (1, 0, 2))
    q_s, k_s, v_s, kb_s, vb_s, ga_s, lru_s, conv_s, hl_s = _proj_call(
        xs2, conv0_s, state_lru[l], p, seq_len=ss, tm=PROJ_TM, feature_major=False)
    ckt = jnp.transpose(cache_k[l], (0, 2, 3, 4, 1)).reshape(bs, N_HEADS, V_DIM, past)
    cv = cache_v[l].reshape(bs, past * N_HEADS, V_DIM)
    o_s = _attn_sample_call(q_s, kb_s, vb_s, ckt, cv, p, batch=bs, seq_len=ss, lam_init=lam_init)
    y_s = _merge_call(xs2, o_s, ga_s, lru_s, p, tm=MERGE_TM)

    k_prompt = jnp.transpose(kt_p.reshape(bp, N_HEADS, 2, HEAD_DIM, sp), (0, 4, 1, 2, 3))
    return (
        y_p.reshape(bp, sp, D_MODEL),
        y_s.reshape(bs, ss, D_MODEL),
        k_prompt[None],
        v_p.reshape(1, bp, sp, N_HEADS, V_DIM),
        jnp.transpose(conv_p, (1, 0, 2))[None],
        hl_p[None],
        k_s.reshape(1, bs, ss, N_HEADS, 2, HEAD_DIM),
        v_s.reshape(1, bs, ss, N_HEADS, V_DIM),
        jnp.transpose(conv_s, (1, 0, 2))[None],
        hl_s[None],
    )
```
